```python
import jax
import jax.numpy as jnp
from jax import lax
import numpy as np

D_MODEL = 2048
BATCH = 4
SEQ = 4096
DEPTH = 4

CHUNK = 64
N_META = 16
SB_BLOCK = 128
NORM_EPS = 1e-6
D_FF = 256 * ((8 * D_MODEL // 3 + 255) // 256)

HG_KEY = 128
HG_VAL = 128
HG_WIDTH = D_MODEL // 4
HG_HEADS = HG_WIDTH // HG_VAL
HG_QK = HG_HEADS * HG_KEY

SB_HEAD_DIM = 128
SB_WIDTH = D_MODEL // 2
SB_HEADS = SB_WIDTH // SB_HEAD_DIM
SB_SCALE = SB_HEAD_DIM ** -0.5

RW_HEAD_DIM = 64
RW_WIDTH = D_MODEL - HG_WIDTH - SB_WIDTH
RW_HEADS = RW_WIDTH // RW_HEAD_DIM
RW_DECAY_LORA = max(32, int(round(1.8 * D_MODEL ** 0.5 / 32)) * 32)
RW_AAA_LORA = max(32, int(round(1.8 * D_MODEL ** 0.5 / 32)) * 32)
RW_MV_LORA = max(32, int(round(1.3 * D_MODEL ** 0.5 / 32)) * 32)
RW_GATE_LORA = max(32, int(round(0.6 * D_MODEL ** 0.8 / 32)) * 32)
RW_LN_EPS = 64e-5

MIX_WIDTH = HG_WIDTH + SB_WIDTH + RW_WIDTH
IN_SIZES = (HG_QK, HG_QK, HG_WIDTH, HG_WIDTH, SB_WIDTH, SB_WIDTH, SB_WIDTH,
            RW_WIDTH, RW_WIDTH, RW_WIDTH, RW_DECAY_LORA, RW_AAA_LORA, RW_GATE_LORA)
N_IN = sum(IN_SIZES)
RW_SHIFT = 3 * RW_WIDTH + RW_DECAY_LORA + RW_AAA_LORA + RW_GATE_LORA
RW_START = N_IN - RW_SHIFT

kernel_name = 'hybrid_hgrn2_stickbreak_rwkv7_macaron'


def rms_norm(x, w):
    xf = x.astype(jnp.float32)
    y = xf * lax.rsqrt(jnp.mean(xf * xf, axis=-1, keepdims=True) + NORM_EPS)
    return (y * w.astype(jnp.float32)).astype(x.dtype)


def swiglu(x, wi, wo):
    gate, up = jnp.split(x @ wi, 2, axis=-1)
    return (jax.nn.silu(gate) * up) @ wo


def token_shift(p, mu):
    prev = jnp.pad(p, ((0, 0), (1, 0), (0, 0)))[:, :-1]
    return p + (prev - p) * mu


def split_sizes(t, sizes):
    idx = [int(v) for v in np.cumsum(sizes)[:-1]]
    return jnp.split(t, idx, axis=-1)


def hgrn2_mixer(q, f_logit, i, g, lb, norm_w):
    f32 = jnp.float32
    B, L, _ = q.shape
    n_chunks = L // CHUNK
    fz = f_logit.astype(f32)
    lb = lb.astype(f32)
    log_f = jnp.logaddexp(jnp.log(lb), jnp.log1p(-lb) + jax.nn.log_sigmoid(fz))
    k = (1.0 - lb) * jax.nn.sigmoid(-fz)
    q = jax.nn.silu(q.astype(f32))

    def chunks(t, d):
        return t.reshape(B, n_chunks, CHUNK, HG_HEADS, d).transpose(1, 0, 3, 2, 4)

    causal = jnp.tril(jnp.ones((CHUNK, CHUNK), dtype=bool))[:, :, None]

    def step(S, inp):
        qc, kc, gc, vc = inp
        G = jnp.cumsum(gc, axis=2)
        rel = jnp.exp(jnp.where(causal, G[:, :, :, None, :] - G[:, :, None, :, :], -jnp.inf))
        att = jnp.einsum('bhtk,bhsk,bhtsk->bhts', qc, kc, rel)
        o = (jnp.einsum('bhts,bhsv->bhtv', att, vc)
             + jnp.einsum('bhtk,bhkv->bhtv', qc * jnp.exp(G), S))
        G_end = G[:, :, -1:, :]
        S = (jnp.exp(G_end[:, :, 0, :, None]) * S
             + jnp.einsum('bhsk,bhsv->bhkv', kc * jnp.exp(G_end - G), vc))
        return S, o

    S0 = jnp.zeros((B, HG_HEADS, HG_KEY, HG_VAL), f32)
    _, o = lax.scan(step, S0, (chunks(q, HG_KEY), chunks(k, HG_KEY), chunks(log_f, HG_KEY),
                               chunks(i.astype(f32), HG_VAL)))
    o = o.transpose(1, 0, 3, 2, 4).reshape(B, L, HG_HEADS, HG_VAL)
    o = rms_norm(o, norm_w) * jax.nn.silu(g.astype(f32).reshape(B, L, HG_HEADS, HG_VAL))
    return o.reshape(B, L, HG_WIDTH)


def stick_breaking_mixer(q, k, v, qn_w, kn_w, on_w):
    f32 = jnp.float32
    B, L, _ = q.shape

    def heads(t):
        return t.reshape(B, L, SB_HEADS, SB_HEAD_DIM)

    qh = rms_norm(heads(q), qn_w).astype(f32).transpose(0, 2, 1, 3) * SB_SCALE
    kh = rms_norm(heads(k), kn_w).astype(f32).transpose(0, 2, 1, 3)
    vh = heads(v).astype(f32).transpose(0, 2, 1, 3)
    outs = []
    for blk in range(L // SB_BLOCK):
        q0, q1 = blk * SB_BLOCK, (blk + 1) * SB_BLOCK
        z = jnp.einsum('bhqd,bhkd->bhqk', qh[:, :, q0:q1], kh[:, :, :q1])
        strict = jnp.arange(q1)[None, :] < jnp.arange(q0, q1)[:, None]
        log_1mb = jnp.where(strict, jax.nn.log_sigmoid(-z), 0.0)
        between = lax.cumsum(log_1mb, axis=3, reverse=True) - log_1mb
        A = jnp.where(strict, jnp.exp(jax.nn.log_sigmoid(z) + between), 0.0)
        outs.append(jnp.einsum('bhqk,bhkd->bhqd', A, vh[:, :, :q1]))
    o = jnp.concatenate(outs, axis=2).transpose(0, 2, 1, 3)
    return rms_norm(o, on_w).reshape(B, L, SB_WIDTH)


def rwkv7_mixer(r, k, v, w_lo, a_lo, g_lo, w0, w2, a0, a2, g2, k_k, k_a, r_k, ln_w, ln_b,
                v_first, v_res):
    f32 = jnp.float32
    B, L, _ = r.shape
    r, k, v = r.astype(f32), k.astype(f32), v.astype(f32)
    w = -jax.nn.softplus(-(w0.astype(f32) + jnp.tanh(w_lo.astype(f32)) @ w2.astype(f32))) - 0.5
    decay = jnp.exp(-jnp.exp(w))
    a = jax.nn.sigmoid(a0.astype(f32) + a_lo.astype(f32) @ a2.astype(f32))
    g = jax.nn.sigmoid(g_lo.astype(f32)) @ g2.astype(f32)
    if v_res is None:
        v_first = v
    else:
        v_lo, v0, v2 = v_res
        v = v + (v_first - v) * jax.nn.sigmoid(v0.astype(f32) + v_lo.astype(f32) @ v2.astype(f32))

    def hd(t):
        return t.reshape(B, L, RW_HEADS, RW_HEAD_DIM)

    kk = hd(k * k_k.astype(f32))
    kk = kk * lax.rsqrt(jnp.maximum(jnp.sum(kk * kk, axis=-1, keepdims=True), 1e-24))
    k = k * (1.0 + (a - 1.0) * k_a.astype(f32))
    rh, kh, vh, wh = hd(r), hd(k), hd(v), hd(decay)
    bh = kk * hd(a)

    def step(S, inp):
        r_t, w_t, k_t, v_t, kk_t, b_t = inp
        sa = -jnp.einsum('bhvk,bhk->bhv', S, kk_t)
        S = (S * w_t[:, :, None, :] + sa[..., None] * b_t[:, :, None, :]
             + v_t[..., None] * k_t[:, :, None, :])
        return S, jnp.einsum('bhvk,bhk->bhv', S, r_t)

    S0 = jnp.zeros((B, RW_HEADS, RW_HEAD_DIM, RW_HEAD_DIM), f32)
    seq = tuple(jnp.moveaxis(t, 1, 0) for t in (rh, wh, kh, vh, kk, bh))
    _, y = lax.scan(step, S0, seq)
    y = jnp.moveaxis(y, 0, 1)
    mu = jnp.mean(y, axis=-1, keepdims=True)
    var = jnp.mean(jnp.square(y - mu), axis=-1, keepdims=True)
    yn = ((y - mu) * lax.rsqrt(var + RW_LN_EPS)).reshape(B, L, RW_WIDTH)
    yn = yn * ln_w.astype(f32) + ln_b.astype(f32)
    bonus = jnp.sum(rh * kh * r_k.astype(f32), axis=-1, keepdims=True) * vh
    out = (yn + bonus.reshape(B, L, RW_WIDTH)) * g
    return out, v_first


def setup_inputs(seed: int = 0) -> dict:
    key = jax.random.key(seed)
    ks = iter(jax.random.split(key, 40))
    f32 = jnp.float32
    D = D_MODEL

    def nrm(shape, scale):
        return jax.random.normal(next(ks), shape, f32) * scale

    def gain(shape):
        return 1.0 + 0.05 * jax.random.normal(next(ks), shape, f32)

    def unif(shape, lo, hi):
        return jax.random.uniform(next(ks), shape, f32, lo, hi)

    return {
        'x': nrm((BATCH, SEQ, D), 1.0),
        'meta': nrm((N_META, D), 1.0),
        'norm_ffn1': gain((DEPTH, D)),
        'ffn1_wi': nrm((DEPTH, D, 2 * D_FF), D ** -0.5),
        'ffn1_wo': nrm((DEPTH, D_FF, D), D_FF ** -0.5),
        'norm_mix': gain((DEPTH, D)),
        'w_in': nrm((DEPTH, D, N_IN), D ** -0.5),
        'w_in_v': nrm((DEPTH - 1, D, RW_MV_LORA), D ** -0.5),
        'hg_lb': nrm((DEPTH, HG_QK), 1.0),
        'hg_norm': gain((DEPTH, HG_VAL)),
        'sb_qn': gain((DEPTH, SB_HEAD_DIM)),
        'sb_kn': gain((DEPTH, SB_HEAD_DIM)),
        'sb_on': gain((DEPTH, SB_HEAD_DIM)),
        'rw_mu': unif((DEPTH, RW_SHIFT), 0.0, 1.0),
        'rw_mu_v': unif((DEPTH - 1, RW_MV_LORA), 0.0, 1.0),
        'rw_w0': unif((DEPTH, RW_WIDTH), -6.5, -1.5),
        'rw_w2': nrm((DEPTH, RW_DECAY_LORA, RW_WIDTH), 0.1 * RW_DECAY_LORA ** -0.5),
        'rw_a0': nrm((DEPTH, RW_WIDTH), 0.1),
        'rw_a2': nrm((DEPTH, RW_AAA_LORA, RW_WIDTH), RW_AAA_LORA ** -0.5),
        'rw_g2': nrm((DEPTH, RW_GATE_LORA, RW_WIDTH), RW_GATE_LORA ** -0.5),
        'rw_v0': 1.0 + nrm((DEPTH - 1, RW_WIDTH), 0.1),
        'rw_v2': nrm((DEPTH - 1, RW_MV_LORA, RW_WIDTH), RW_MV_LORA ** -0.5),
        'rw_kk': 0.85 + nrm((DEPTH, RW_WIDTH), 0.05),
        'rw_ka': 1.0 + nrm((DEPTH, RW_WIDTH), 0.05),
        'rw_rk': nrm((DEPTH, RW_HEADS, RW_HEAD_DIM), 0.1),
        'rw_ln_w': gain((DEPTH, RW_WIDTH)),
        'rw_ln_b': nrm((DEPTH, RW_WIDTH), 0.02),
        'w_out': nrm((DEPTH, MIX_WIDTH, D), MIX_WIDTH ** -0.5),
        'norm_ffn2': gain((DEPTH, D)),
        'ffn2_wi': nrm((DEPTH, D, 2 * D_FF), D ** -0.5),
        'ffn2_wo': nrm((DEPTH, D_FF, D), D_FF ** -0.5),
    }


def reference(x, meta, norm_ffn1, ffn1_wi, ffn1_wo, norm_mix, w_in, w_in_v, hg_lb, hg_norm,
              sb_qn, sb_kn, sb_on, rw_mu, rw_mu_v, rw_w0, rw_w2, rw_a0, rw_a2, rw_g2, rw_v0, rw_v2,
              rw_kk, rw_ka, rw_rk, rw_ln_w, rw_ln_b, w_out, norm_ffn2, ffn2_wi, ffn2_wo):
    B, S, D = x.shape
    L_real = N_META + S
    pad = (-L_real) % SB_BLOCK
    h = jnp.concatenate([jnp.broadcast_to(meta.astype(x.dtype)[None], (B, N_META, D)), x], axis=1)
    h = jnp.pad(h, ((0, 0), (0, pad), (0, 0)))

    lb_all = jnp.cumsum(jax.nn.softmax(hg_lb.astype(jnp.float32), axis=0), axis=0)
    lb_all = lb_all - lb_all[0]

    v_first = None
    for l in range(DEPTH):
        h = h + 0.5 * swiglu(rms_norm(h, norm_ffn1[l]), ffn1_wi[l], ffn1_wo[l])

        u = rms_norm(h, norm_mix[l])
        w_l = w_in[l] if l == 0 else jnp.concatenate([w_in[l], w_in_v[l - 1]], axis=1)
        proj = u @ w_l
        hq, hf, hi, hg, sq, sk, sv = split_sizes(proj[..., :RW_START], IN_SIZES[:7])
        rr, rk, rv, rwl, ral, rgl = split_sizes(token_shift(proj[..., RW_START:N_IN], rw_mu[l]),
                                                IN_SIZES[7:])
        o_hg = hgrn2_mixer(hq, hf, hi, hg, lb_all[l], hg_norm[l])
        o_sb = stick_breaking_mixer(sq, sk, sv, sb_qn[l], sb_kn[l], sb_on[l])
        if l == 0:
            v_res = None
        else:
            v_res = (token_shift(proj[..., N_IN:], rw_mu_v[l - 1]), rw_v0[l - 1], rw_v2[l - 1])
        o_rw, v_first = rwkv7_mixer(rr, rk, rv, rwl, ral, rgl, rw_w0[l], rw_w2[l], rw_a0[l], rw_a2[l],
                                    rw_g2[l], rw_kk[l], rw_ka[l], rw_rk[l], rw_ln_w[l], rw_ln_b[l],
                                    v_first, v_res)
        mix = jnp.concatenate([o_hg.astype(h.dtype), o_sb.astype(h.dtype), o_rw.astype(h.dtype)], axis=-1)
        h = h + mix @ w_out[l]

        h = h + 0.5 * swiglu(rms_norm(h, norm_ffn2[l]), ffn2_wi[l], ffn2_wo[l])

    return h[:, N_META:L_real]
```

```python
import functools

import jax
import jax.numpy as jnp
from jax import lax
from jax.experimental import pallas as pl
from jax.experimental.pallas import tpu as pltpu

F32 = jnp.float32
BF16 = jnp.bfloat16

D_MODEL = 2048
DEPTH = 4
CHUNK = 64
N_META = 16
SB_BLOCK = 128
NORM_EPS = 1e-6
D_FF = 256 * ((8 * D_MODEL // 3 + 255) // 256)

HG_KEY = 128
HG_WIDTH = D_MODEL // 4
HG_HEADS = HG_WIDTH // HG_KEY

SB_HEAD_DIM = 128
SB_WIDTH = D_MODEL // 2
SB_HEADS = SB_WIDTH // SB_HEAD_DIM
SB_SCALE = SB_HEAD_DIM ** -0.5

RW_HEAD_DIM = 64
RW_WIDTH = D_MODEL - HG_WIDTH - SB_WIDTH
RW_HEADS = RW_WIDTH // RW_HEAD_DIM
RW_PAIRS = RW_HEADS // 2
RW_DECAY_LORA = max(32, int(round(1.8 * D_MODEL ** 0.5 / 32)) * 32)
RW_AAA_LORA = max(32, int(round(1.8 * D_MODEL ** 0.5 / 32)) * 32)
RW_MV_LORA = max(32, int(round(1.3 * D_MODEL ** 0.5 / 32)) * 32)
RW_GATE_LORA = max(32, int(round(0.6 * D_MODEL ** 0.8 / 32)) * 32)
RW_LN_EPS = 64e-5
RW_LORA_W = RW_DECAY_LORA + RW_AAA_LORA + RW_GATE_LORA + RW_MV_LORA

N_IN = 4 * HG_WIDTH + 3 * SB_WIDTH + 3 * RW_WIDTH + RW_DECAY_LORA + RW_AAA_LORA + RW_GATE_LORA
N_PROJ = N_IN + RW_MV_LORA
RW_START = 4 * HG_WIDTH + 3 * SB_WIDTH

LANES = 128
VMEM_LIMIT = 56 * 1024 * 1024
NEG_BIG = -1e30


def _cparams(*sem):
    return pltpu.CompilerParams(dimension_semantics=sem, vmem_limit_bytes=VMEM_LIMIT)


def _row_tile(t, want):
    tm = min(want, t)
    while t % tm:
        tm //= 2
    return tm


def _split2(x):
    hi = x.astype(BF16)
    lo = (x - hi.astype(F32)).astype(BF16)
    return hi, lo


def _split3(x):
    hi = x.astype(BF16)
    r = x - hi.astype(F32)
    mid = r.astype(BF16)
    lo = (r - mid.astype(F32)).astype(BF16)
    return hi, mid, lo


def _dot(a, b):
    return jnp.dot(a, b, preferred_element_type=F32)


def _dot_t(a, b):
    return lax.dot_general(a, b, (((1,), (1,)), ((), ())), preferred_element_type=F32)


def _dot3(x, m):
    hi, mid, lo = _split3(x)
    return _dot(hi, m) + _dot(mid, m) + _dot(lo, m)


def _log_sigmoid(z):
    return jnp.minimum(z, 0.0) - jnp.log1p(jnp.exp(-jnp.abs(z)))


def _rms(x, w):
    ms = jnp.mean(x * x, axis=-1, keepdims=True)
    return x * lax.rsqrt(ms + NORM_EPS) * w


def _ffn_kernel(x_ref, nw_ref, wg_ref, wu_ref, wo_ref, o_ref, xn_ref, acc_ref):
    j = pl.program_id(1)

    @pl.when(j == 0)
    def _():
        xn_ref[...] = _rms(x_ref[...], nw_ref[...]).astype(BF16)
        acc_ref[...] = jnp.zeros_like(acc_ref)

    xn = xn_ref[...]
    g = _dot(xn, wg_ref[...])
    u = _dot(xn, wu_ref[...])
    a = (g * jax.nn.sigmoid(g)) * u
    acc_ref[...] += _dot(a.astype(BF16), wo_ref[...])

    @pl.when(j == pl.num_programs(1) - 1)
    def _():
        o_ref[...] = x_ref[...] + 0.5 * acc_ref[...]


def _ffn(h, nw, wi, wo):
    t = h.shape[0]
    tm = _row_tile(t, 512)
    tf = 512
    nf = D_FF // tf
    return pl.pallas_call(
        _ffn_kernel,
        grid=(t // tm, nf),
        in_specs=[
            pl.BlockSpec((tm, D_MODEL), lambda i, j: (i, 0)),
            pl.BlockSpec((1, D_MODEL), lambda i, j: (0, 0)),
            pl.BlockSpec((D_MODEL, tf), lambda i, j: (0, j)),
            pl.BlockSpec((D_MODEL, tf), lambda i, j: (0, j + nf)),
            pl.BlockSpec((tf, D_MODEL), lambda i, j: (j, 0)),
        ],
        out_specs=pl.BlockSpec((tm, D_MODEL), lambda i, j: (i, 0)),
        out_shape=jax.ShapeDtypeStruct((t, D_MODEL), F32),
        scratch_shapes=[pltpu.VMEM((tm, D_MODEL), BF16), pltpu.VMEM((tm, D_MODEL), F32)],
        compiler_params=_cparams("parallel", "arbitrary"),
        name="ffn",
    )(h, nw.reshape(1, D_MODEL), wi, wi, wo)


def _inproj_kernel(x_ref, nw_ref, w_ref, o_ref, xn_ref):
    @pl.when(pl.program_id(1) == 0)
    def _():
        xn_ref[...] = _rms(x_ref[...], nw_ref[...]).astype(BF16)

    o_ref[...] = _dot(xn_ref[...], w_ref[...])


def _inproj(h, nw, w):
    t = h.shape[0]
    tm = _row_tile(t, 512)
    tn = 1024
    return pl.pallas_call(
        _inproj_kernel,
        grid=(t // tm, N_PROJ // tn),
        in_specs=[
            pl.BlockSpec((tm, D_MODEL), lambda i, j: (i, 0)),
            pl.BlockSpec((1, D_MODEL), lambda i, j: (0, 0)),
            pl.BlockSpec((D_MODEL, tn), lambda i, j: (0, j)),
        ],
        out_specs=pl.BlockSpec((tm, tn), lambda i, j: (i, j)),
        out_shape=jax.ShapeDtypeStruct((t, N_PROJ), F32),
        scratch_shapes=[pltpu.VMEM((tm, D_MODEL), BF16)],
        compiler_params=_cparams("parallel", "arbitrary"),
        name="inproj",
    )(h, nw.reshape(1, D_MODEL), w)


def _outproj_kernel(h_ref, a0_ref, a1_ref, a2_ref, w0_ref, w1_ref, w2_ref, o_ref):
    acc = _dot(a0_ref[...], w0_ref[...])
    acc += _dot(a1_ref[...], w1_ref[...])
    acc += _dot(a2_ref[...], w2_ref[...])
    o_ref[...] = h_ref[...] + acc


def _outproj(h, a_hg, a_sb, a_rw, w0, w1, w2):
    t = h.shape[0]
    tm = _row_tile(t, 512)
    row = lambda i: (i, 0)
    fixed = lambda i: (0, 0)
    return pl.pallas_call(
        _outproj_kernel,
        grid=(t // tm,),
        in_specs=[
            pl.BlockSpec((tm, D_MODEL), row),
            pl.BlockSpec((tm, HG_WIDTH), row),
            pl.BlockSpec((tm, SB_WIDTH), row),
            pl.BlockSpec((tm, RW_WIDTH), row),
            pl.BlockSpec((HG_WIDTH, D_MODEL), fixed),
            pl.BlockSpec((SB_WIDTH, D_MODEL), fixed),
            pl.BlockSpec((RW_WIDTH, D_MODEL), fixed),
        ],
        out_specs=pl.BlockSpec((tm, D_MODEL), row),
        out_shape=jax.ShapeDtypeStruct((t, D_MODEL), F32),
        compiler_params=_cparams("parallel"),
        name="outproj",
    )(h, a_hg, a_sb, a_rw, w0, w1, w2)


HG_SUB = 16


def _hgrn2_kernel(q_ref, f_ref, i_ref, g_ref, lbp_ref, nw_ref, o_ref, s_ref, *, layer, n_chunks):
    @pl.when(pl.program_id(1) == 0)
    def _():
        s_ref[...] = jnp.zeros_like(s_ref)

    if layer > 0:
        p = lbp_ref[...]
        e = jnp.exp(p - jnp.max(p, axis=0, keepdims=True))
        sm = e / jnp.sum(e, axis=0, keepdims=True)
        lb_all = sm[1:2, :]
        for r in range(2, layer + 1):
            lb_all = lb_all + sm[r:r + 1, :]
        log_lb_all = jnp.log(lb_all)
        log_1mlb_all = jnp.log1p(-lb_all)

    row = lax.broadcasted_iota(jnp.int32, (CHUNK, CHUNK), 0)
    col = lax.broadcasted_iota(jnp.int32, (CHUNK, CHUNK), 1)
    tri = (row >= col).astype(BF16)
    srow = lax.broadcasted_iota(jnp.int32, (HG_SUB, 1), 0)
    scol = lax.broadcasted_iota(jnp.int32, (HG_SUB, CHUNK), 1)
    nw = nw_ref[...]
    n_sub = CHUNK // HG_SUB

    def chunk(c, carry):
        r0 = pl.multiple_of(c * CHUNK, CHUNK)
        rows = pl.ds(r0, CHUNK)
        for h in range(HG_HEADS):
            cols = slice(h * HG_KEY, (h + 1) * HG_KEY)
            z = f_ref[rows, cols]
            ls = _log_sigmoid(z)
            if layer > 0:
                a = log_lb_all[:, cols]
                cc = log_1mlb_all[:, cols] + ls
                log_f = jnp.maximum(a, cc) + jnp.log1p(jnp.exp(-jnp.abs(a - cc)))
                k = (1.0 - lb_all[:, cols]) * jax.nn.sigmoid(-z)
            else:
                log_f = ls
                k = jax.nn.sigmoid(-z)
            qv = q_ref[rows, cols]
            q = qv * jax.nn.sigmoid(qv)
            v = i_ref[rows, cols]
            f_hi, f_mid, f_lo = _split3(log_f)
            G = _dot(tri, f_hi) + _dot(tri, f_mid) + _dot(tri, f_lo)
            g_end = G[CHUNK - 1:CHUNK, :]
            st = s_ref[h]
            vb = v.astype(BF16)
            kb = k.astype(BF16)

            o = _dot_t((q * jnp.exp(G)).astype(BF16), st.astype(BF16))

            att_rows = []
            for i in range(n_sub):
                sl = slice(i * HG_SUB, (i + 1) * HG_SUB)
                G_i = G[sl, :]
                q_i = q[sl, :]
                att_i = jnp.zeros((HG_SUB, CHUNK), F32)
                if i > 0:
                    g_b = G[i * HG_SUB - 1:i * HG_SUB, :]
                    qd = (q_i * jnp.exp(G_i - g_b)).astype(BF16)
                    kd = (k * jnp.exp(jnp.minimum(g_b - G, 0.0))).astype(BF16)
                    att_i = jnp.where(scol < i * HG_SUB, _dot_t(qd, kd), 0.0)
                p_all = []
                for s in range(HG_SUB):
                    d = jnp.where(srow >= s, G_i - G_i[s:s + 1, :], NEG_BIG)
                    p_all.append((q_i * jnp.exp(d)).astype(BF16))
                pk = _dot_t(jnp.concatenate(p_all, axis=0), kb)
                for s in range(HG_SUB):
                    att_i = att_i + jnp.where(scol == i * HG_SUB + s, pk[s * HG_SUB:(s + 1) * HG_SUB], 0.0)
                att_rows.append(att_i)
            att = jnp.concatenate(att_rows, axis=0)
            o = o + _dot(att.astype(BF16), vb)

            kd_end = (k * jnp.exp(g_end - G)).astype(BF16)
            upd = lax.dot_general(vb, kd_end, (((0,), (0,)), ((), ())), preferred_element_type=F32)
            s_ref[h] = st * jnp.exp(g_end) + upd

            gv = g_ref[rows, cols]
            o_ref[rows, cols] = (_rms(o, nw) * (gv * jax.nn.sigmoid(gv))).astype(o_ref.dtype)
        return carry

    lax.fori_loop(0, n_chunks, chunk, 0)


def _hgrn2(proj, hg_lb, norm_w, layer, batch, seq):
    lb_rows = _row_tile(seq, 256)
    nblk = seq // lb_rows
    spec = lambda cb: pl.BlockSpec((lb_rows, HG_WIDTH), lambda b, i, cb=cb: (b * nblk + i, cb))
    return pl.pallas_call(
        functools.partial(_hgrn2_kernel, layer=layer, n_chunks=lb_rows // CHUNK),
        grid=(batch, nblk),
        in_specs=[spec(0), spec(1), spec(2), spec(3),
                  pl.BlockSpec((DEPTH, HG_WIDTH), lambda b, i: (0, 0)),
                  pl.BlockSpec((1, HG_KEY), lambda b, i: (0, 0))],
        out_specs=pl.BlockSpec((lb_rows, HG_WIDTH), lambda b, i: (b * nblk + i, 0)),
        out_shape=jax.ShapeDtypeStruct((batch * seq, HG_WIDTH), BF16),
        scratch_shapes=[pltpu.VMEM((HG_HEADS, HG_KEY, HG_KEY), F32)],
        compiler_params=_cparams("parallel", "arbitrary"),
        name="hgrn2",
    )(proj, proj, proj, proj, hg_lb, norm_w.reshape(1, HG_KEY))


def _sbprep_kernel(q_ref, k_ref, v_ref, qn_ref, kn_ref, qo_ref, ko_ref, vo_ref):
    qn = qn_ref[...]
    kn = kn_ref[...]
    for h in range(SB_HEADS):
        cols = slice(h * SB_HEAD_DIM, (h + 1) * SB_HEAD_DIM)
        qo_ref[:, cols] = (_rms(q_ref[:, cols], qn) * SB_SCALE).astype(BF16)
        ko_ref[:, cols] = _rms(k_ref[:, cols], kn).astype(BF16)
    vo_ref[...] = v_ref[...].astype(BF16)


def _sbprep(proj, qn_w, kn_w):
    t = proj.shape[0]
    tm = _row_tile(t, 512)
    spec = lambda cb: pl.BlockSpec((tm, SB_WIDTH), lambda i, cb=cb: (i, cb))
    wspec = pl.BlockSpec((1, SB_HEAD_DIM), lambda i: (0, 0))
    out = jax.ShapeDtypeStruct((t, SB_WIDTH), BF16)
    c0 = 4 * HG_WIDTH // SB_WIDTH
    return pl.pallas_call(
        _sbprep_kernel,
        grid=(t // tm,),
        in_specs=[spec(c0), spec(c0 + 1), spec(c0 + 2), wspec, wspec],
        out_specs=[spec(0), spec(0), spec(0)],
        out_shape=[out, out, out],
        compiler_params=_cparams("parallel"),
        name="sbprep",
    )(proj, proj, proj, qn_w.reshape(1, -1), kn_w.reshape(1, -1))


def _sbattn_kernel(q_ref, k_ref, v_ref, on_ref, o_ref):
    i = pl.program_id(2)
    q = q_ref[...]
    row = lax.broadcasted_iota(jnp.int32, (SB_BLOCK, SB_BLOCK), 0)
    col = lax.broadcasted_iota(jnp.int32, (SB_BLOCK, SB_BLOCK), 1)
    mrow = lax.broadcasted_iota(jnp.int32, (SB_BLOCK, 2 * SB_BLOCK), 0)
    mcol = lax.broadcasted_iota(jnp.int32, (SB_BLOCK, 2 * SB_BLOCK), 1)
    m2 = ((mrow > mcol) | (mcol >= SB_BLOCK)).astype(BF16)

    def body(jj, carry):
        acc, cb = carry
        j = i - jj
        rows = pl.ds(pl.multiple_of(j * SB_BLOCK, SB_BLOCK), SB_BLOCK)
        z = _dot_t(q, k_ref[rows, :])
        sp = jnp.log1p(jnp.exp(-jnp.abs(z)))
        log_b = jnp.minimum(z, 0.0) - sp
        keep = col < row + jj * SB_BLOCK
        l1 = jnp.where(keep, -jnp.maximum(z, 0.0) - sp, 0.0)
        hi, lo = _split2(l1)
        r = _dot(hi, m2) + _dot(lo, m2)
        a = jnp.where(keep, jnp.exp(log_b + r[:, :SB_BLOCK] + cb), 0.0)
        acc = acc + _dot(a.astype(BF16), v_ref[rows, :])
        return acc, cb + r[:, SB_BLOCK:]

    zero = jnp.zeros((SB_BLOCK, SB_BLOCK), F32)
    acc, _ = lax.fori_loop(0, i + 1, body, (zero, zero))
    o_ref[...] = _rms(acc, on_ref[...]).astype(o_ref.dtype)


def _sbattn(q, k, v, on_w, batch, seq):
    nq = seq // SB_BLOCK
    blk = pl.BlockSpec((SB_BLOCK, SB_HEAD_DIM), lambda b, h, i: (b * nq + i, h))
    full = pl.BlockSpec((seq, SB_HEAD_DIM), lambda b, h, i: (b, h))
    return pl.pallas_call(
        _sbattn_kernel,
        grid=(batch, SB_HEADS, nq),
        in_specs=[blk, full, full, pl.BlockSpec((1, SB_HEAD_DIM), lambda b, h, i: (0, 0))],
        out_specs=blk,
        out_shape=jax.ShapeDtypeStruct((batch * seq, SB_WIDTH), BF16),
        compiler_params=_cparams("parallel", "parallel", "arbitrary"),
        name="sbattn",
    )(q, k, v, on_w.reshape(1, -1))


def _seg_matrix(n, seg):
    r = lax.broadcasted_iota(jnp.int32, (n, n), 0) // seg
    c = lax.broadcasted_iota(jnp.int32, (n, n), 1) // seg
    return (r == c).astype(BF16)


def _rwprep_kernel(*refs, layer):
    if layer > 0:
        (r_ref, k_ref, v_ref, lo_ref, mu_ref, w0_ref, w2_ref, a0_ref, a2_ref, g2_ref, kk_ref, ka_ref,
         v0_ref, v2_ref, vf_ref,
         ro_ref, wo_ref, ko_ref, vo_ref, kko_ref, bo_ref, go_ref, prev_ref) = refs
    else:
        (r_ref, k_ref, v_ref, lo_ref, mu_ref, w0_ref, w2_ref, a0_ref, a2_ref, g2_ref, kk_ref, ka_ref,
         ro_ref, wo_ref, ko_ref, vo_ref, kko_ref, bo_ref, go_ref, prev_ref) = refs

    @pl.when(pl.program_id(1) == 0)
    def _():
        prev_ref[...] = jnp.zeros_like(prev_ref)

    n = r_ref.shape[0]
    first = lax.broadcasted_iota(jnp.int32, (n, 1), 0) == 0

    def shifted(ref, part):
        p = ref[...]
        cols = slice(part * RW_WIDTH, (part + 1) * RW_WIDTH)
        prev = jnp.where(first, prev_ref[0:1, cols], pltpu.roll(p, 1, axis=0))
        prev_ref[0:1, cols] = p[n - 1:n, :]
        return p + (prev - p) * mu_ref[:, cols]

    r = shifted(r_ref, 0)
    k = shifted(k_ref, 1)
    v = shifted(v_ref, 2)
    lo = shifted(lo_ref, 3)
    lo_b = lo.astype(BF16)

    wl = -(w0_ref[...] + _dot(jnp.tanh(lo).astype(BF16), w2_ref[...]))
    w = -(jnp.maximum(wl, 0.0) + jnp.log1p(jnp.exp(-jnp.abs(wl)))) - 0.5
    decay = jnp.exp(-jnp.exp(w))
    a = jax.nn.sigmoid(a0_ref[...] + _dot(lo_b, a2_ref[...]))
    g = _dot(jax.nn.sigmoid(lo).astype(BF16), g2_ref[...])
    if layer > 0:
        v_first = jnp.concatenate([vf_ref[p] for p in range(RW_PAIRS)], axis=1)
        v = v + (v_first - v) * jax.nn.sigmoid(v0_ref[...] + _dot(lo_b, v2_ref[...]))

    seg = _seg_matrix(RW_WIDTH, RW_HEAD_DIM)
    kk = k * kk_ref[...]
    kk = kk * lax.rsqrt(jnp.maximum(_dot3(kk * kk, seg), 1e-24))
    k = k * (1.0 + (a - 1.0) * ka_ref[...])
    b = kk * a

    for p in range(RW_PAIRS):
        cols = slice(p * LANES, (p + 1) * LANES)
        ro_ref[p] = r[:, cols]
        wo_ref[p] = decay[:, cols]
        ko_ref[p] = k[:, cols]
        vo_ref[p] = v[:, cols]
        kko_ref[p] = kk[:, cols]
        bo_ref[p] = b[:, cols]
        go_ref[p] = g[:, cols]


def _rwprep(proj, mu, w0, w2p, a0, a2p, g2p, k_k, k_a, v_res, layer, batch, seq):
    lb_rows = _row_tile(seq, 256)
    nblk = seq // lb_rows
    c0 = RW_START // RW_WIDTH
    spec = lambda cb: pl.BlockSpec((lb_rows, RW_WIDTH), lambda b, i, cb=cb: (b * nblk + i, cb))
    pspec = pl.BlockSpec((RW_PAIRS, lb_rows, LANES), lambda b, i: (0, b * nblk + i, 0))
    vec = pl.BlockSpec((1, RW_WIDTH), lambda b, i: (0, 0))
    mat = pl.BlockSpec((RW_LORA_W, RW_WIDTH), lambda b, i: (0, 0))
    in_specs = [spec(c0), spec(c0 + 1), spec(c0 + 2), spec(c0 + 3),
                pl.BlockSpec((1, 4 * RW_WIDTH), lambda b, i: (0, 0)),
                vec, mat, vec, mat, mat, vec, vec]
    args = [proj, proj, proj, proj, mu, w0, w2p, a0, a2p, g2p, k_k, k_a]
    if layer > 0:
        v0, v2p, v_first = v_res
        in_specs += [vec, mat, pspec]
        args += [v0, v2p, v_first]
    out = jax.ShapeDtypeStruct((RW_PAIRS, batch * seq, LANES), F32)
    return pl.pallas_call(
        functools.partial(_rwprep_kernel, layer=layer),
        grid=(batch, nblk),
        in_specs=in_specs,
        out_specs=[pspec] * 7,
        out_shape=[out] * 7,
        scratch_shapes=[pltpu.VMEM((8, 4 * RW_WIDTH), F32)],
        compiler_params=_cparams("parallel", "arbitrary"),
        name="rwprep",
    )(*args)


def _rwscan_kernel(r_ref, w_ref, k_ref, v_ref, kk_ref, b_ref, y_ref, s_ref, *, batch, steps):
    @pl.when(pl.program_id(0) == 0)
    def _():
        s_ref[...] = jnp.zeros_like(s_ref)

    seg = _seg_matrix(LANES, RW_HEAD_DIM)
    vi = lax.broadcasted_iota(jnp.int32, (RW_HEAD_DIM, LANES), 0)
    li = lax.broadcasted_iota(jnp.int32, (RW_HEAD_DIM, LANES), 1)
    diag = (li % RW_HEAD_DIM) == vi
    n = RW_HEAD_DIM

    def step(t, carry):
        row = pl.ds(t, 1)
        for p in range(RW_PAIRS):
            for bi in range(batch):
                c = p * batch + bi
                s = s_ref[c]
                v_t = v_ref[p, bi, row, :]
                p1h, p1l = _split2(s * kk_ref[p, bi, row, :])
                v_hi = v_t.astype(BF16).astype(F32)
                a_hi = jnp.where(diag, v_hi, 0.0).astype(BF16)
                a_lo = jnp.where(diag, v_t - v_hi, 0.0).astype(BF16)
                res = _dot(jnp.concatenate([p1h, p1l, a_hi, a_lo], axis=0), seg)
                sa = res[0:n] + res[n:2 * n]
                vb = res[2 * n:3 * n] + res[3 * n:4 * n]
                s2 = s * w_ref[p, bi, row, :] - sa * b_ref[p, bi, row, :] + vb * k_ref[p, bi, row, :]
                s_ref[c] = s2
                p3h, p3l = _split2(s2 * r_ref[p, bi, row, :])
                yres = _dot(jnp.concatenate([p3h, p3l], axis=0), seg)
                yb = yres[0:n] + yres[n:2 * n]
                y_ref[p, bi, row, :] = jnp.sum(jnp.where(diag, yb, 0.0), axis=0, keepdims=True)
        return carry

    lax.fori_loop(0, steps, step, 0)


def _rwscan(r, w, k, v, kk, b, batch, seq):
    tb = _row_tile(seq, 64)
    spec = pl.BlockSpec((RW_PAIRS, batch, tb, LANES), lambda i: (0, 0, i, 0))
    shp = lambda x: x.reshape(RW_PAIRS, batch, seq, LANES)
    y = pl.pallas_call(
        functools.partial(_rwscan_kernel, batch=batch, steps=tb),
        grid=(seq // tb,),
        in_specs=[spec] * 6,
        out_specs=spec,
        out_shape=jax.ShapeDtypeStruct((RW_PAIRS, batch, seq, LANES), F32),
        scratch_shapes=[pltpu.VMEM((batch * RW_PAIRS, RW_HEAD_DIM, LANES), F32)],
        compiler_params=_cparams("arbitrary"),
        name="rwscan",
    )(shp(r), shp(w), shp(k), shp(v), shp(kk), shp(b))
    return y.reshape(RW_PAIRS, batch * seq, LANES)


def _rwpost_kernel(y_ref, r_ref, k_ref, v_ref, g_ref, lnw_ref, lnb_ref, rk_ref, o_ref):
    seg = _seg_matrix(LANES, RW_HEAD_DIM)
    y = y_ref[...]
    mu = _dot3(y, seg) * (1.0 / RW_HEAD_DIM)
    d = y - mu
    var = _dot3(d * d, seg) * (1.0 / RW_HEAD_DIM)
    yn = d * lax.rsqrt(var + RW_LN_EPS) * lnw_ref[...] + lnb_ref[...]
    bonus = _dot3(r_ref[...] * k_ref[...] * rk_ref[...], seg) * v_ref[...]
    o_ref[...] = ((yn + bonus) * g_ref[...]).astype(o_ref.dtype)


def _rwpost(y, r, k, v, g, ln_w, ln_b, r_k):
    t = y.shape[1]
    tm = _row_tile(t, 1024)
    row = pl.BlockSpec((None, tm, LANES), lambda p, i: (p, i, 0))
    vec = pl.BlockSpec((1, LANES), lambda p, i: (0, p))
    return pl.pallas_call(
        _rwpost_kernel,
        grid=(RW_PAIRS, t // tm),
        in_specs=[row] * 5 + [vec] * 3,
        out_specs=pl.BlockSpec((tm, LANES), lambda p, i: (i, p)),
        out_shape=jax.ShapeDtypeStruct((t, RW_WIDTH), BF16),
        compiler_params=_cparams("parallel", "parallel"),
        name="rwpost",
    )(y, r, k, v, g, ln_w, ln_b, r_k)


def _pad_rows(w, start):
    return jnp.pad(w, ((start, RW_LORA_W - start - w.shape[0]), (0, 0))).astype(BF16)


def _trunk(h, batch, seq, p, layers):
    vec = lambda x: x.reshape(1, -1).astype(F32)
    v_first = None
    for l in layers:
        h = _ffn(h, p["norm_ffn1"][l], p["ffn1_wi"][l].astype(BF16), p["ffn1_wo"][l].astype(BF16))

        if l == 0:
            w_l = jnp.pad(p["w_in"][l], ((0, 0), (0, RW_MV_LORA)))
            mu = jnp.pad(p["rw_mu"][l], (0, RW_MV_LORA))
        else:
            w_l = jnp.concatenate([p["w_in"][l], p["w_in_v"][l - 1]], axis=1)
            mu = jnp.concatenate([p["rw_mu"][l], p["rw_mu_v"][l - 1]])
        proj = _inproj(h, p["norm_mix"][l], w_l.astype(BF16))

        o_hg = _hgrn2(proj, p["hg_lb"].astype(F32), p["hg_norm"][l], l, batch, seq)

        q, k, v = _sbprep(proj, p["sb_qn"][l], p["sb_kn"][l])
        o_sb = _sbattn(q, k, v, p["sb_on"][l], batch, seq)

        v_res = None
        if l > 0:
            v_res = (vec(p["rw_v0"][l - 1]),
                     _pad_rows(p["rw_v2"][l - 1], RW_DECAY_LORA + RW_AAA_LORA + RW_GATE_LORA), v_first)
        r, w, k_r, v_r, kk, b, g = _rwprep(
            proj, vec(mu), vec(p["rw_w0"][l]), _pad_rows(p["rw_w2"][l], 0),
            vec(p["rw_a0"][l]), _pad_rows(p["rw_a2"][l], RW_DECAY_LORA),
            _pad_rows(p["rw_g2"][l], RW_DECAY_LORA + RW_AAA_LORA),
            vec(p["rw_kk"][l]), vec(p["rw_ka"][l]), v_res, l, batch, seq)
        if l == 0:
            v_first = v_r
        y = _rwscan(r, w, k_r, v_r, kk, b, batch, seq)
        o_rw = _rwpost(y, r, k_r, v_r, g, vec(p["rw_ln_w"][l]), vec(p["rw_ln_b"][l]), vec(p["rw_rk"][l]))

        w_out = p["w_out"][l].astype(BF16)
        h = _outproj(h, o_hg, o_sb, o_rw, w_out[:HG_WIDTH], w_out[HG_WIDTH:HG_WIDTH + SB_WIDTH],
                     w_out[HG_WIDTH + SB_WIDTH:])

        h = _ffn(h, p["norm_ffn2"][l], p["ffn2_wi"][l].astype(BF16), p["ffn2_wo"][l].astype(BF16))
    return h


def kernel(x, meta, norm_ffn1, ffn1_wi, ffn1_wo, norm_mix, w_in, w_in_v, hg_lb, hg_norm, sb_qn, sb_kn, sb_on, rw_mu, rw_mu_v, rw_w0, rw_w2, rw_a0, rw_a2, rw_g2, rw_v0, rw_v2, rw_kk, rw_ka, rw_rk, rw_ln_w, rw_ln_b, w_out, norm_ffn2, ffn2_wi, ffn2_wo):
    batch, s, d = x.shape
    l_real = N_META + s
    pad = (-l_real) % SB_BLOCK
    seq = l_real + pad
    h = jnp.concatenate([jnp.broadcast_to(meta.astype(x.dtype)[None], (batch, N_META, d)), x], axis=1)
    h = jnp.pad(h, ((0, 0), (0, pad), (0, 0))).reshape(batch * seq, d)
    p = dict(norm_ffn1=norm_ffn1, ffn1_wi=ffn1_wi, ffn1_wo=ffn1_wo, norm_mix=norm_mix, w_in=w_in,
             w_in_v=w_in_v, hg_lb=hg_lb, hg_norm=hg_norm, sb_qn=sb_qn, sb_kn=sb_kn, sb_on=sb_on,
             rw_mu=rw_mu, rw_mu_v=rw_mu_v, rw_w0=rw_w0, rw_w2=rw_w2, rw_a0=rw_a0, rw_a2=rw_a2,
             rw_g2=rw_g2, rw_v0=rw_v0, rw_v2=rw_v2, rw_kk=rw_kk, rw_ka=rw_ka, rw_rk=rw_rk,
             rw_ln_w=rw_ln_w, rw_ln_b=rw_ln_b, w_out=w_out, norm_ffn2=norm_ffn2, ffn2_wi=ffn2_wi,
             ffn2_wo=ffn2_wo)
    h = _trunk(h, batch, seq, p, range(DEPTH))
    return h.reshape(batch, seq, d)[:, N_META:l_real]
```

```python
import functools

import jax
import jax.numpy as jnp
from jax import lax
from jax.experimental import pallas as pl
from jax.experimental.pallas import tpu as pltpu

F32 = jnp.float32
BF16 = jnp.bfloat16

D_MODEL = 2048
DEPTH = 4
CHUNK = 64
N_META = 16
SB_BLOCK = 128
NORM_EPS = 1e-6
D_FF = 256 * ((8 * D_MODEL // 3 + 255) // 256)

HG_KEY = 128
HG_WIDTH = D_MODEL // 4
HG_HEADS = HG_WIDTH // HG_KEY

SB_HEAD_DIM = 128
SB_WIDTH = D_MODEL // 2
SB_HEADS = SB_WIDTH // SB_HEAD_DIM
SB_SCALE = SB_HEAD_DIM ** -0.5

RW_HEAD_DIM = 64
RW_WIDTH = D_MODEL - HG_WIDTH - SB_WIDTH
RW_HEADS = RW_WIDTH // RW_HEAD_DIM
RW_PAIRS = RW_HEADS // 2
RW_DECAY_LORA = max(32, int(round(1.8 * D_MODEL ** 0.5 / 32)) * 32)
RW_AAA_LORA = max(32, int(round(1.8 * D_MODEL ** 0.5 / 32)) * 32)
RW_MV_LORA = max(32, int(round(1.3 * D_MODEL ** 0.5 / 32)) * 32)
RW_GATE_LORA = max(32, int(round(0.6 * D_MODEL ** 0.8 / 32)) * 32)
RW_LN_EPS = 64e-5
RW_LORA_W = RW_DECAY_LORA + RW_AAA_LORA + RW_GATE_LORA + RW_MV_LORA

N_IN = 4 * HG_WIDTH + 3 * SB_WIDTH + 3 * RW_WIDTH + RW_DECAY_LORA + RW_AAA_LORA + RW_GATE_LORA
N_PROJ = N_IN + RW_MV_LORA
RW_START = 4 * HG_WIDTH + 3 * SB_WIDTH

LANES = 128
VMEM_LIMIT = 56 * 1024 * 1024
NEG_BIG = -1e30


def _cparams(*sem):
    return pltpu.CompilerParams(dimension_semantics=sem, vmem_limit_bytes=VMEM_LIMIT)


def _row_tile(t, want):
    tm = min(want, t)
    while t % tm:
        tm //= 2
    return tm


def _split2(x):
    hi = x.astype(BF16)
    lo = (x - hi.astype(F32)).astype(BF16)
    return hi, lo


def _split3(x):
    hi = x.astype(BF16)
    r = x - hi.astype(F32)
    mid = r.astype(BF16)
    lo = (r - mid.astype(F32)).astype(BF16)
    return hi, mid, lo


def _dot(a, b):
    return jnp.dot(a, b, preferred_element_type=F32)


def _dot_t(a, b):
    return lax.dot_general(a, b, (((1,), (1,)), ((), ())), preferred_element_type=F32)


def _dot3(x, m):
    hi, mid, lo = _split3(x)
    return _dot(hi, m) + _dot(mid, m) + _dot(lo, m)


def _log_sigmoid(z):
    return jnp.minimum(z, 0.0) - jnp.log1p(jnp.exp(-jnp.abs(z)))


def _rms(x, w):
    ms = jnp.mean(x * x, axis=-1, keepdims=True)
    return x * lax.rsqrt(ms + NORM_EPS) * w


def _ffn_kernel(x_ref, nw_ref, wg_ref, wu_ref, wo_ref, o_ref, xn_ref, acc_ref):
    j = pl.program_id(1)

    @pl.when(j == 0)
    def _():
        xn_ref[...] = _rms(x_ref[...], nw_ref[...]).astype(BF16)
        acc_ref[...] = jnp.zeros_like(acc_ref)

    xn = xn_ref[...]
    g = _dot(xn, wg_ref[...])
    u = _dot(xn, wu_ref[...])
    a = (g * jax.nn.sigmoid(g)) * u
    acc_ref[...] += _dot(a.astype(BF16), wo_ref[...])

    @pl.when(j == pl.num_programs(1) - 1)
    def _():
        o_ref[...] = x_ref[...] + 0.5 * acc_ref[...]


def _ffn(h, nw, wi, wo):
    t = h.shape[0]
    tm = _row_tile(t, 512)
    tf = 512
    nf = D_FF // tf
    return pl.pallas_call(
        _ffn_kernel,
        grid=(t // tm, nf),
        in_specs=[
            pl.BlockSpec((tm, D_MODEL), lambda i, j: (i, 0)),
            pl.BlockSpec((1, D_MODEL), lambda i, j: (0, 0)),
            pl.BlockSpec((D_MODEL, tf), lambda i, j: (0, j)),
            pl.BlockSpec((D_MODEL, tf), lambda i, j: (0, j + nf)),
            pl.BlockSpec((tf, D_MODEL), lambda i, j: (j, 0)),
        ],
        out_specs=pl.BlockSpec((tm, D_MODEL), lambda i, j: (i, 0)),
        out_shape=jax.ShapeDtypeStruct((t, D_MODEL), F32),
        scratch_shapes=[pltpu.VMEM((tm, D_MODEL), BF16), pltpu.VMEM((tm, D_MODEL), F32)],
        compiler_params=_cparams("parallel", "arbitrary"),
        name="ffn",
    )(h, nw.reshape(1, D_MODEL), wi, wi, wo)


def _inproj_kernel(x_ref, nw_ref, w_ref, o_ref, xn_ref):
    @pl.when(pl.program_id(1) == 0)
    def _():
        xn_ref[...] = _rms(x_ref[...], nw_ref[...]).astype(BF16)

    o_ref[...] = _dot(xn_ref[...], w_ref[...])


def _inproj(h, nw, w):
    t = h.shape[0]
    tm = _row_tile(t, 512)
    tn = 1024
    return pl.pallas_call(
        _inproj_kernel,
        grid=(t // tm, N_PROJ // tn),
        in_specs=[
            pl.BlockSpec((tm, D_MODEL), lambda i, j: (i, 0)),
            pl.BlockSpec((1, D_MODEL), lambda i, j: (0, 0)),
            pl.BlockSpec((D_MODEL, tn), lambda i, j: (0, j)),
        ],
        out_specs=pl.BlockSpec((tm, tn), lambda i, j: (i, j)),
        out_shape=jax.ShapeDtypeStruct((t, N_PROJ), F32),
        scratch_shapes=[pltpu.VMEM((tm, D_MODEL), BF16)],
        compiler_params=_cparams("parallel", "arbitrary"),
        name="inproj",
    )(h, nw.reshape(1, D_MODEL), w)


def _outproj_kernel(h_ref, a0_ref, a1_ref, a2_ref, w0_ref, w1_ref, w2_ref, o_ref):
    acc = _dot(a0_ref[...], w0_ref[...])
    acc += _dot(a1_ref[...], w1_ref[...])
    acc += _dot(a2_ref[...], w2_ref[...])
    o_ref[...] = h_ref[...] + acc


def _outproj(h, a_hg, a_sb, a_rw, w0, w1, w2):
    t = h.shape[0]
    tm = _row_tile(t, 512)
    row = lambda i: (i, 0)
    fixed = lambda i: (0, 0)
    return pl.pallas_call(
        _outproj_kernel,
        grid=(t // tm,),
        in_specs=[
            pl.BlockSpec((tm, D_MODEL), row),
            pl.BlockSpec((tm, HG_WIDTH), row),
            pl.BlockSpec((tm, SB_WIDTH), row),
            pl.BlockSpec((tm, RW_WIDTH), row),
            pl.BlockSpec((HG_WIDTH, D_MODEL), fixed),
            pl.BlockSpec((SB_WIDTH, D_MODEL), fixed),
            pl.BlockSpec((RW_WIDTH, D_MODEL), fixed),
        ],
        out_specs=pl.BlockSpec((tm, D_MODEL), row),
        out_shape=jax.ShapeDtypeStruct((t, D_MODEL), F32),
        compiler_params=_cparams("parallel"),
        name="outproj",
    )(h, a_hg, a_sb, a_rw, w0, w1, w2)


HG_SUB = 16


def _hgrn2_kernel(q_ref, f_ref, i_ref, g_ref, lbp_ref, nw_ref, o_ref, s_ref, *, layer, n_chunks):
    @pl.when(pl.program_id(1) == 0)
    def _():
        s_ref[...] = jnp.zeros_like(s_ref)

    if layer > 0:
        p = lbp_ref[...]
        e = jnp.exp(p - jnp.max(p, axis=0, keepdims=True))
        sm = e / jnp.sum(e, axis=0, keepdims=True)
        lb_all = sm[1:2, :]
        for r in range(2, layer + 1):
            lb_all = lb_all + sm[r:r + 1, :]
        log_lb_all = jnp.log(lb_all)
        log_1mlb_all = jnp.log1p(-lb_all)

    row = lax.broadcasted_iota(jnp.int32, (CHUNK, CHUNK), 0)
    col = lax.broadcasted_iota(jnp.int32, (CHUNK, CHUNK), 1)
    tri = (row >= col).astype(BF16)
    srow = lax.broadcasted_iota(jnp.int32, (HG_SUB, 1), 0)
    scol = lax.broadcasted_iota(jnp.int32, (HG_SUB, CHUNK), 1)
    nw = nw_ref[...]
    n_sub = CHUNK // HG_SUB

    def chunk(c, carry):
        r0 = pl.multiple_of(c * CHUNK, CHUNK)
        rows = pl.ds(r0, CHUNK)
        for h in range(HG_HEADS):
            cols = slice(h * HG_KEY, (h + 1) * HG_KEY)
            z = f_ref[rows, cols]
            ls = _log_sigmoid(z)
            if layer > 0:
                a = log_lb_all[:, cols]
                cc = log_1mlb_all[:, cols] + ls
                log_f = jnp.maximum(a, cc) + jnp.log1p(jnp.exp(-jnp.abs(a - cc)))
                k = (1.0 - lb_all[:, cols]) * jax.nn.sigmoid(-z)
            else:
                log_f = ls
                k = jax.nn.sigmoid(-z)
            qv = q_ref[rows, cols]
            q = qv * jax.nn.sigmoid(qv)
            v = i_ref[rows, cols]
            f_hi, f_mid, f_lo = _split3(log_f)
            G = _dot(tri, f_hi) + _dot(tri, f_mid) + _dot(tri, f_lo)
            g_end = G[CHUNK - 1:CHUNK, :]
            st = s_ref[h]
            vb = v.astype(BF16)
            kb = k.astype(BF16)

            o = _dot_t((q * jnp.exp(G)).astype(BF16), st.astype(BF16))

            att_rows = []
            for i in range(n_sub):
                sl = slice(i * HG_SUB, (i + 1) * HG_SUB)
                G_i = G[sl, :]
                q_i = q[sl, :]
                att_i = jnp.zeros((HG_SUB, CHUNK), F32)
                if i > 0:
                    g_b = G[i * HG_SUB - 1:i * HG_SUB, :]
                    qd = (q_i * jnp.exp(G_i - g_b)).astype(BF16)
                    kd = (k * jnp.exp(jnp.minimum(g_b - G, 0.0))).astype(BF16)
                    att_i = jnp.where(scol < i * HG_SUB, _dot_t(qd, kd), 0.0)
                p_all = []
                for s in range(HG_SUB):
                    d = jnp.where(srow >= s, G_i - G_i[s:s + 1, :], NEG_BIG)
                    p_all.append((q_i * jnp.exp(d)).astype(BF16))
                pk = _dot_t(jnp.concatenate(p_all, axis=0), kb)
                for s in range(HG_SUB):
                    att_i = att_i + jnp.where(scol == i * HG_SUB + s, pk[s * HG_SUB:(s + 1) * HG_SUB], 0.0)
                att_rows.append(att_i)
            att = jnp.concatenate(att_rows, axis=0)
            o = o + _dot(att.astype(BF16), vb)

            kd_end = (k * jnp.exp(g_end - G)).astype(BF16)
            upd = lax.dot_general(vb, kd_end, (((0,), (0,)), ((), ())), preferred_element_type=F32)
            s_ref[h] = st * jnp.exp(g_end) + upd

            gv = g_ref[rows, cols]
            o_ref[rows, cols] = (_rms(o, nw) * (gv * jax.nn.sigmoid(gv))).astype(o_ref.dtype)
        return carry

    lax.fori_loop(0, n_chunks, chunk, 0)


def _hgrn2(proj, hg_lb, norm_w, layer, batch, seq):
    lb_rows = _row_tile(seq, 256)
    nblk = seq // lb_rows
    spec = lambda cb: pl.BlockSpec((lb_rows, HG_WIDTH), lambda b, i, cb=cb: (b * nblk + i, cb))
    return pl.pallas_call(
        functools.partial(_hgrn2_kernel, layer=layer, n_chunks=lb_rows // CHUNK),
        grid=(batch, nblk),
        in_specs=[spec(0), spec(1), spec(2), spec(3),
                  pl.BlockSpec((DEPTH, HG_WIDTH), lambda b, i: (0, 0)),
                  pl.BlockSpec((1, HG_KEY), lambda b, i: (0, 0))],
        out_specs=pl.BlockSpec((lb_rows, HG_WIDTH), lambda b, i: (b * nblk + i, 0)),
        out_shape=jax.ShapeDtypeStruct((batch * seq, HG_WIDTH), BF16),
        scratch_shapes=[pltpu.VMEM((HG_HEADS, HG_KEY, HG_KEY), F32)],
        compiler_params=_cparams("parallel", "arbitrary"),
        name="hgrn2",
    )(proj, proj, proj, proj, hg_lb, norm_w.reshape(1, HG_KEY))


def _sbprep_kernel(q_ref, k_ref, v_ref, qn_ref, kn_ref, qo_ref, ko_ref, vo_ref):
    qn = qn_ref[...]
    kn = kn_ref[...]
    for h in range(SB_HEADS):
        cols = slice(h * SB_HEAD_DIM, (h + 1) * SB_HEAD_DIM)
        qo_ref[:, cols] = (_rms(q_ref[:, cols], qn) * SB_SCALE).astype(BF16)
        ko_ref[:, cols] = _rms(k_ref[:, cols], kn).astype(BF16)
    vo_ref[...] = v_ref[...].astype(BF16)


def _sbprep(proj, qn_w, kn_w):
    t = proj.shape[0]
    tm = _row_tile(t, 512)
    spec = lambda cb: pl.BlockSpec((tm, SB_WIDTH), lambda i, cb=cb: (i, cb))
    wspec = pl.BlockSpec((1, SB_HEAD_DIM), lambda i: (0, 0))
    out = jax.ShapeDtypeStruct((t, SB_WIDTH), BF16)
    c0 = 4 * HG_WIDTH // SB_WIDTH
    return pl.pallas_call(
        _sbprep_kernel,
        grid=(t // tm,),
        in_specs=[spec(c0), spec(c0 + 1), spec(c0 + 2), wspec, wspec],
        out_specs=[spec(0), spec(0), spec(0)],
        out_shape=[out, out, out],
        compiler_params=_cparams("parallel"),
        name="sbprep",
    )(proj, proj, proj, qn_w.reshape(1, -1), kn_w.reshape(1, -1))


def _sbattn_kernel(q_ref, k_ref, v_ref, on_ref, o_ref, *, tq):
    i = pl.program_id(2)
    q = q_ref[...]
    groups = tq // SB_BLOCK
    row = lax.broadcasted_iota(jnp.int32, (tq, SB_BLOCK), 0)
    col = lax.broadcasted_iota(jnp.int32, (tq, SB_BLOCK), 1)
    last = col == SB_BLOCK - 1
    mr = lax.broadcasted_iota(jnp.int32, (SB_BLOCK, SB_BLOCK), 0)
    mc = lax.broadcasted_iota(jnp.int32, (SB_BLOCK, SB_BLOCK), 1)
    m1 = ((mr > mc) | (mc == SB_BLOCK - 1)).astype(BF16)

    def slab(kb, acc, cb, diagonal):
        rows = pl.ds(pl.multiple_of(kb * tq, SB_BLOCK), tq)
        z = _dot_t(q, k_ref[rows, :])
        parts = [None] * groups
        for u in range(groups - 1, -1, -1):
            zu = z[:, u * SB_BLOCK:(u + 1) * SB_BLOCK]
            sp = jnp.log(1.0 + jnp.exp(-jnp.abs(zu)))
            log_b = jnp.minimum(zu, 0.0) - sp
            l1 = log_b - zu
            if diagonal:
                keep = col + u * SB_BLOCK < row
                l1 = jnp.where(keep, l1, 0.0)
            hi, lo = _split2(l1)
            r = _dot(hi, m1) + _dot(lo, m1)
            total = jnp.broadcast_to(r[:, SB_BLOCK - 1:SB_BLOCK], r.shape)
            a = jnp.exp(log_b + jnp.where(last, 0.0, r) + cb)
            if diagonal:
                a = jnp.where(keep, a, 0.0)
            parts[u] = a.astype(BF16)
            cb = cb + total
        acc = acc + _dot(jnp.concatenate(parts, axis=1), v_ref[rows, :])
        return acc, cb

    zero = jnp.zeros((tq, SB_HEAD_DIM), F32)
    acc, cb = slab(i, zero, zero, True)
    acc, _ = lax.fori_loop(0, i, lambda jj, c: slab(i - 1 - jj, c[0], c[1], False), (acc, cb))
    o_ref[...] = _rms(acc, on_ref[...]).astype(o_ref.dtype)


def _sbattn(q, k, v, on_w, batch, seq):
    tq = 3 * SB_BLOCK if seq % (3 * SB_BLOCK) == 0 else SB_BLOCK
    nq = seq // tq
    blk = pl.BlockSpec((tq, SB_HEAD_DIM), lambda b, h, i: (b * nq + i, h))
    full = pl.BlockSpec((seq, SB_HEAD_DIM), lambda b, h, i: (b, h))
    return pl.pallas_call(
        functools.partial(_sbattn_kernel, tq=tq),
        grid=(batch, SB_HEADS, nq),
        in_specs=[blk, full, full, pl.BlockSpec((1, SB_HEAD_DIM), lambda b, h, i: (0, 0))],
        out_specs=blk,
        out_shape=jax.ShapeDtypeStruct((batch * seq, SB_WIDTH), BF16),
        compiler_params=_cparams("parallel", "parallel", "arbitrary"),
        name="sbattn",
    )(q, k, v, on_w.reshape(1, -1))


def _seg_matrix(n, seg):
    r = lax.broadcasted_iota(jnp.int32, (n, n), 0) // seg
    c = lax.broadcasted_iota(jnp.int32, (n, n), 1) // seg
    return (r == c).astype(BF16)


def _rwprep_kernel(*refs, layer):
    if layer > 0:
        (r_ref, k_ref, v_ref, lo_ref, mu_ref, w0_ref, w2_ref, a0_ref, a2_ref, g2_ref, kk_ref, ka_ref,
         v0_ref, v2_ref, vf_ref,
         ro_ref, wo_ref, ko_ref, vo_ref, kko_ref, bo_ref, go_ref, prev_ref) = refs
    else:
        (r_ref, k_ref, v_ref, lo_ref, mu_ref, w0_ref, w2_ref, a0_ref, a2_ref, g2_ref, kk_ref, ka_ref,
         ro_ref, wo_ref, ko_ref, vo_ref, kko_ref, bo_ref, go_ref, prev_ref) = refs

    @pl.when(pl.program_id(1) == 0)
    def _():
        prev_ref[...] = jnp.zeros_like(prev_ref)

    n = r_ref.shape[0]
    first = lax.broadcasted_iota(jnp.int32, (n, 1), 0) == 0

    def shifted(ref, part):
        p = ref[...]
        cols = slice(part * RW_WIDTH, (part + 1) * RW_WIDTH)
        prev = jnp.where(first, prev_ref[0:1, cols], pltpu.roll(p, 1, axis=0))
        prev_ref[0:1, cols] = p[n - 1:n, :]
        return p + (prev - p) * mu_ref[:, cols]

    r = shifted(r_ref, 0)
    k = shifted(k_ref, 1)
    v = shifted(v_ref, 2)
    lo = shifted(lo_ref, 3)
    lo_b = lo.astype(BF16)

    wl = -(w0_ref[...] + _dot(jnp.tanh(lo).astype(BF16), w2_ref[...]))
    w = -(jnp.maximum(wl, 0.0) + jnp.log1p(jnp.exp(-jnp.abs(wl)))) - 0.5
    decay = jnp.exp(-jnp.exp(w))
    a = jax.nn.sigmoid(a0_ref[...] + _dot(lo_b, a2_ref[...]))
    g = _dot(jax.nn.sigmoid(lo).astype(BF16), g2_ref[...])
    if layer > 0:
        v_first = jnp.concatenate([vf_ref[p] for p in range(RW_PAIRS)], axis=1)
        v = v + (v_first - v) * jax.nn.sigmoid(v0_ref[...] + _dot(lo_b, v2_ref[...]))

    seg = _seg_matrix(RW_WIDTH, RW_HEAD_DIM)
    kk = k * kk_ref[...]
    kk = kk * lax.rsqrt(jnp.maximum(_dot3(kk * kk, seg), 1e-24))
    k = k * (1.0 + (a - 1.0) * ka_ref[...])
    b = kk * a

    for p in range(RW_PAIRS):
        cols = slice(p * LANES, (p + 1) * LANES)
        ro_ref[p] = r[:, cols]
        wo_ref[p] = decay[:, cols]
        ko_ref[p] = k[:, cols]
        vo_ref[p] = v[:, cols]
        kko_ref[p] = kk[:, cols]
        bo_ref[p] = b[:, cols]
        go_ref[p] = g[:, cols]


def _rwprep(proj, mu, w0, w2p, a0, a2p, g2p, k_k, k_a, v_res, layer, batch, seq):
    lb_rows = _row_tile(seq, 256)
    nblk = seq // lb_rows
    c0 = RW_START // RW_WIDTH
    spec = lambda cb: pl.BlockSpec((lb_rows, RW_WIDTH), lambda b, i, cb=cb: (b * nblk + i, cb))
    pspec = pl.BlockSpec((RW_PAIRS, lb_rows, LANES), lambda b, i: (0, b * nblk + i, 0))
    vec = pl.BlockSpec((1, RW_WIDTH), lambda b, i: (0, 0))
    mat = pl.BlockSpec((RW_LORA_W, RW_WIDTH), lambda b, i: (0, 0))
    in_specs = [spec(c0), spec(c0 + 1), spec(c0 + 2), spec(c0 + 3),
                pl.BlockSpec((1, 4 * RW_WIDTH), lambda b, i: (0, 0)),
                vec, mat, vec, mat, mat, vec, vec]
    args = [proj, proj, proj, proj, mu, w0, w2p, a0, a2p, g2p, k_k, k_a]
    if layer > 0:
        v0, v2p, v_first = v_res
        in_specs += [vec, mat, pspec]
        args += [v0, v2p, v_first]
    out = jax.ShapeDtypeStruct((RW_PAIRS, batch * seq, LANES), F32)
    return pl.pallas_call(
        functools.partial(_rwprep_kernel, layer=layer),
        grid=(batch, nblk),
        in_specs=in_specs,
        out_specs=[pspec] * 7,
        out_shape=[out] * 7,
        scratch_shapes=[pltpu.VMEM((8, 4 * RW_WIDTH), F32)],
        compiler_params=_cparams("parallel", "arbitrary"),
        name="rwprep",
    )(*args)


def _rwscan_kernel(r_ref, w_ref, k_ref, v_ref, kk_ref, b_ref, y_ref, s_ref, *, batch, steps):
    @pl.when(pl.program_id(0) == 0)
    def _():
        s_ref[...] = jnp.zeros_like(s_ref)

    seg = _seg_matrix(LANES, RW_HEAD_DIM)
    vi = lax.broadcasted_iota(jnp.int32, (RW_HEAD_DIM, LANES), 0)
    li = lax.broadcasted_iota(jnp.int32, (RW_HEAD_DIM, LANES), 1)
    diag = (li % RW_HEAD_DIM) == vi
    n = RW_HEAD_DIM

    chains = [(p, bi) for p in range(RW_PAIRS) for bi in range(batch)]

    def y_row(yb):
        return jnp.sum(jnp.where(diag, yb, 0.0), axis=0, keepdims=True)

    def step(t, carry):
        row = pl.ds(t, 1)
        prow = pl.ds(jnp.maximum(t - 1, 0), 1)
        states, res = [], []
        for c, (p, bi) in enumerate(chains):
            s = s_ref[c]
            p1h, p1l = _split2(s * kk_ref[p, bi, row, :])
            a = jnp.where(diag, v_ref[p, bi, row, :], 0.0).astype(BF16)
            p3 = (s * r_ref[p, bi, prow, :]).astype(BF16)
            states.append(s)
            res.append(_dot(jnp.concatenate([p1h, p1l, a, p3], axis=0), seg))
        for c, (p, bi) in enumerate(chains):
            rc = res[c]
            sa = rc[0:n] + rc[n:2 * n]
            vb = rc[2 * n:3 * n]
            y_ref[p, bi, prow, :] = y_row(rc[3 * n:4 * n])
            s_ref[c] = (states[c] * w_ref[p, bi, row, :] - sa * b_ref[p, bi, row, :]
                        + vb * k_ref[p, bi, row, :])
        return carry

    lax.fori_loop(0, steps, step, 0)

    last = pl.ds(steps - 1, 1)
    for c, (p, bi) in enumerate(chains):
        yb = _dot((s_ref[c] * r_ref[p, bi, last, :]).astype(BF16), seg)
        y_ref[p, bi, last, :] = y_row(yb)


def _rwscan(r, w, k, v, kk, b, batch, seq):
    tb = _row_tile(seq, 64)
    spec = pl.BlockSpec((RW_PAIRS, batch, tb, LANES), lambda i: (0, 0, i, 0))
    shp = lambda x: x.reshape(RW_PAIRS, batch, seq, LANES)
    y = pl.pallas_call(
        functools.partial(_rwscan_kernel, batch=batch, steps=tb),
        grid=(seq // tb,),
        in_specs=[spec] * 6,
        out_specs=spec,
        out_shape=jax.ShapeDtypeStruct((RW_PAIRS, batch, seq, LANES), F32),
        scratch_shapes=[pltpu.VMEM((batch * RW_PAIRS, RW_HEAD_DIM, LANES), F32)],
        compiler_params=_cparams("arbitrary"),
        name="rwscan",
    )(shp(r), shp(w), shp(k), shp(v), shp(kk), shp(b))
    return y.reshape(RW_PAIRS, batch * seq, LANES)


def _rwpost_kernel(y_ref, r_ref, k_ref, v_ref, g_ref, lnw_ref, lnb_ref, rk_ref, o_ref):
    seg = _seg_matrix(LANES, RW_HEAD_DIM)
    y = y_ref[...]
    mu = _dot3(y, seg) * (1.0 / RW_HEAD_DIM)
    d = y - mu
    var = _dot3(d * d, seg) * (1.0 / RW_HEAD_DIM)
    yn = d * lax.rsqrt(var + RW_LN_EPS) * lnw_ref[...] + lnb_ref[...]
    bonus = _dot3(r_ref[...] * k_ref[...] * rk_ref[...], seg) * v_ref[...]
    o_ref[...] = ((yn + bonus) * g_ref[...]).astype(o_ref.dtype)


def _rwpost(y, r, k, v, g, ln_w, ln_b, r_k):
    t = y.shape[1]
    tm = _row_tile(t, 1024)
    row = pl.BlockSpec((None, tm, LANES), lambda p, i: (p, i, 0))
    vec = pl.BlockSpec((1, LANES), lambda p, i: (0, p))
    return pl.pallas_call(
        _rwpost_kernel,
        grid=(RW_PAIRS, t // tm),
        in_specs=[row] * 5 + [vec] * 3,
        out_specs=pl.BlockSpec((tm, LANES), lambda p, i: (i, p)),
        out_shape=jax.ShapeDtypeStruct((t, RW_WIDTH), BF16),
        compiler_params=_cparams("parallel", "parallel"),
        name="rwpost",
    )(y, r, k, v, g, ln_w, ln_b, r_k)


def _pad_rows(w, start):
    return jnp.pad(w, ((start, RW_LORA_W - start - w.shape[0]), (0, 0))).astype(BF16)


def _trunk(h, batch, seq, p, layers):
    vec = lambda x: x.reshape(1, -1).astype(F32)
    v_first = None
    for l in layers:
        h = _ffn(h, p["norm_ffn1"][l], p["ffn1_wi"][l].astype(BF16), p["ffn1_wo"][l].astype(BF16))

        if l == 0:
            w_l = jnp.pad(p["w_in"][l], ((0, 0), (0, RW_MV_LORA)))
            mu = jnp.pad(p["rw_mu"][l], (0, RW_MV_LORA))
        else:
            w_l = jnp.concatenate([p["w_in"][l], p["w_in_v"][l - 1]], axis=1)
            mu = jnp.concatenate([p["rw_mu"][l], p["rw_mu_v"][l - 1]])
        proj = _inproj(h, p["norm_mix"][l], w_l.astype(BF16))

        o_hg = _hgrn2(proj, p["hg_lb"].astype(F32), p["hg_norm"][l], l, batch, seq)

        q, k, v = _sbprep(proj, p["sb_qn"][l], p["sb_kn"][l])
        o_sb = _sbattn(q, k, v, p["sb_on"][l], batch, seq)

        v_res = None
        if l > 0:
            v_res = (vec(p["rw_v0"][l - 1]),
                     _pad_rows(p["rw_v2"][l - 1], RW_DECAY_LORA + RW_AAA_LORA + RW_GATE_LORA), v_first)
        r, w, k_r, v_r, kk, b, g = _rwprep(
            proj, vec(mu), vec(p["rw_w0"][l]), _pad_rows(p["rw_w2"][l], 0),
            vec(p["rw_a0"][l]), _pad_rows(p["rw_a2"][l], RW_DECAY_LORA),
            _pad_rows(p["rw_g2"][l], RW_DECAY_LORA + RW_AAA_LORA),
            vec(p["rw_kk"][l]), vec(p["rw_ka"][l]), v_res, l, batch, seq)
        if l == 0:
            v_first = v_r
        y = _rwscan(r, w, k_r, v_r, kk, b, batch, seq)
        o_rw = _rwpost(y, r, k_r, v_r, g, vec(p["rw_ln_w"][l]), vec(p["rw_ln_b"][l]), vec(p["rw_rk"][l]))

        w_out = p["w_out"][l].astype(BF16)
        h = _outproj(h, o_hg, o_sb, o_rw, w_out[:HG_WIDTH], w_out[HG_WIDTH:HG_WIDTH + SB_WIDTH],
                     w_out[HG_WIDTH + SB_WIDTH:])

        h = _ffn(h, p["norm_ffn2"][l], p["ffn2_wi"][l].astype(BF16), p["ffn2_wo"][l].astype(BF16))
    return h


def kernel(x, meta, norm_ffn1, ffn1_wi, ffn1_wo, norm_mix, w_in, w_in_v, hg_lb, hg_norm, sb_qn, sb_kn, sb_on, rw_mu, rw_mu_v, rw_w0, rw_w2, rw_a0, rw_a2, rw_g2, rw_v0, rw_v2, rw_kk, rw_ka, rw_rk, rw_ln_w, rw_ln_b, w_out, norm_ffn2, ffn2_wi, ffn2_wo):
    batch, s, d = x.shape
    l_real = N_META + s
    pad = (-l_real) % SB_BLOCK
    seq = l_real + pad
    h = jnp.concatenate([jnp.broadcast_to(meta.astype(x.dtype)[None], (batch, N_META, d)), x], axis=1)
    h = jnp.pad(h, ((0, 0), (0, pad), (0, 0))).reshape(batch * seq, d)
    p = dict(norm_ffn1=norm_ffn1, ffn1_wi=ffn1_wi, ffn1_wo=ffn1_wo, norm_mix=norm_mix, w_in=w_in,
             w_in_v=w_in_v, hg_lb=hg_lb, hg_norm=hg_norm, sb_qn=sb_qn, sb_kn=sb_kn, sb_on=sb_on,
             rw_mu=rw_mu, rw_mu_v=rw_mu_v, rw_w0=rw_w0, rw_w2=rw_w2, rw_a0=rw_a0, rw_a2=rw_a2,
             rw_g2=rw_g2, rw_v0=rw_v0, rw_v2=rw_v2, rw_kk=rw_kk, rw_ka=rw_ka, rw_rk=rw_rk,
             rw_ln_w=rw_ln_w, rw_ln_b=rw_ln_b, w_out=w_out, norm_ffn2=norm_ffn2, ffn2_wi=ffn2_wi,
             ffn2_wo=ffn2_wo)
    h = _trunk(h, batch, seq, p, range(DEPTH))
    return h.reshape(batch, seq, d)[:, N_META:l_real]
```

```python
import functools

import jax
import jax.numpy as jnp
from jax import lax
from jax.experimental import pallas as pl
from jax.experimental.pallas import tpu as pltpu

F32 = jnp.float32
BF16 = jnp.bfloat16

D_MODEL = 2048
DEPTH = 4
CHUNK = 64
N_META = 16
SB_BLOCK = 128
NORM_EPS = 1e-6
D_FF = 256 * ((8 * D_MODEL // 3 + 255) // 256)

HG_KEY = 128
HG_WIDTH = D_MODEL // 4
HG_HEADS = HG_WIDTH // HG_KEY

SB_HEAD_DIM = 128
SB_WIDTH = D_MODEL // 2
SB_HEADS = SB_WIDTH // SB_HEAD_DIM
SB_SCALE = SB_HEAD_DIM ** -0.5
SB_HPS = 2

RW_HEAD_DIM = 64
RW_WIDTH = D_MODEL - HG_WIDTH - SB_WIDTH
RW_HEADS = RW_WIDTH // RW_HEAD_DIM
RW_PAIRS = RW_HEADS // 2
RW_UNROLL = 4
RW_DECAY_LORA = max(32, int(round(1.8 * D_MODEL ** 0.5 / 32)) * 32)
RW_AAA_LORA = max(32, int(round(1.8 * D_MODEL ** 0.5 / 32)) * 32)
RW_MV_LORA = max(32, int(round(1.3 * D_MODEL ** 0.5 / 32)) * 32)
RW_GATE_LORA = max(32, int(round(0.6 * D_MODEL ** 0.8 / 32)) * 32)
RW_LN_EPS = 64e-5
RW_LORA_W = RW_DECAY_LORA + RW_AAA_LORA + RW_GATE_LORA + RW_MV_LORA

N_IN = 4 * HG_WIDTH + 3 * SB_WIDTH + 3 * RW_WIDTH + RW_DECAY_LORA + RW_AAA_LORA + RW_GATE_LORA
N_PROJ = N_IN + RW_MV_LORA
RW_START = 4 * HG_WIDTH + 3 * SB_WIDTH

LANES = 128
VMEM_LIMIT = 56 * 1024 * 1024
NEG_BIG = -1e30


def _cparams(*sem):
    return pltpu.CompilerParams(dimension_semantics=sem, vmem_limit_bytes=VMEM_LIMIT)


def _row_tile(t, want):
    tm = min(want, t)
    while t % tm:
        tm //= 2
    return tm


def _split2(x):
    hi = x.astype(BF16)
    lo = (x - hi.astype(F32)).astype(BF16)
    return hi, lo


def _split3(x):
    hi = x.astype(BF16)
    r = x - hi.astype(F32)
    mid = r.astype(BF16)
    lo = (r - mid.astype(F32)).astype(BF16)
    return hi, mid, lo


def _dot(a, b):
    return jnp.dot(a, b, preferred_element_type=F32)


def _dot_t(a, b):
    return lax.dot_general(a, b, (((1,), (1,)), ((), ())), preferred_element_type=F32)


def _dot3(x, m):
    hi, mid, lo = _split3(x)
    return _dot(hi, m) + _dot(mid, m) + _dot(lo, m)


def _log_sigmoid(z):
    return jnp.minimum(z, 0.0) - jnp.log1p(jnp.exp(-jnp.abs(z)))


def _rms(x, w):
    ms = jnp.mean(x * x, axis=-1, keepdims=True)
    return x * lax.rsqrt(ms + NORM_EPS) * w


def _ffn_kernel(x_ref, nw_ref, wg_ref, wu_ref, wo_ref, o_ref, xn_ref, acc_ref):
    j = pl.program_id(1)

    @pl.when(j == 0)
    def _():
        xn_ref[...] = _rms(x_ref[...], nw_ref[...]).astype(BF16)
        acc_ref[...] = jnp.zeros_like(acc_ref)

    xn = xn_ref[...]
    g = _dot(xn, wg_ref[...])
    u = _dot(xn, wu_ref[...])
    a = (g * jax.nn.sigmoid(g)) * u
    acc_ref[...] += _dot(a.astype(BF16), wo_ref[...])

    @pl.when(j == pl.num_programs(1) - 1)
    def _():
        o_ref[...] = x_ref[...] + 0.5 * acc_ref[...]


def _ffn(h, nw, wi, wo):
    t = h.shape[0]
    tm = _row_tile(t, 512)
    tf = 512
    nf = D_FF // tf
    return pl.pallas_call(
        _ffn_kernel,
        grid=(t // tm, nf),
        in_specs=[
            pl.BlockSpec((tm, D_MODEL), lambda i, j: (i, 0)),
            pl.BlockSpec((1, D_MODEL), lambda i, j: (0, 0)),
            pl.BlockSpec((D_MODEL, tf), lambda i, j: (0, j)),
            pl.BlockSpec((D_MODEL, tf), lambda i, j: (0, j + nf)),
            pl.BlockSpec((tf, D_MODEL), lambda i, j: (j, 0)),
        ],
        out_specs=pl.BlockSpec((tm, D_MODEL), lambda i, j: (i, 0)),
        out_shape=jax.ShapeDtypeStruct((t, D_MODEL), F32),
        scratch_shapes=[pltpu.VMEM((tm, D_MODEL), BF16), pltpu.VMEM((tm, D_MODEL), F32)],
        compiler_params=_cparams("parallel", "arbitrary"),
        name="ffn",
    )(h, nw.reshape(1, D_MODEL), wi, wi, wo)


def _inproj_kernel(x_ref, nw_ref, w_ref, o_ref, xn_ref):
    @pl.when(pl.program_id(1) == 0)
    def _():
        xn_ref[...] = _rms(x_ref[...], nw_ref[...]).astype(BF16)

    o_ref[...] = _dot(xn_ref[...], w_ref[...])


def _inproj(h, nw, w):
    t = h.shape[0]
    tm = _row_tile(t, 512)
    tn = 1024
    return pl.pallas_call(
        _inproj_kernel,
        grid=(t // tm, N_PROJ // tn),
        in_specs=[
            pl.BlockSpec((tm, D_MODEL), lambda i, j: (i, 0)),
            pl.BlockSpec((1, D_MODEL), lambda i, j: (0, 0)),
            pl.BlockSpec((D_MODEL, tn), lambda i, j: (0, j)),
        ],
        out_specs=pl.BlockSpec((tm, tn), lambda i, j: (i, j)),
        out_shape=jax.ShapeDtypeStruct((t, N_PROJ), F32),
        scratch_shapes=[pltpu.VMEM((tm, D_MODEL), BF16)],
        compiler_params=_cparams("parallel", "arbitrary"),
        name="inproj",
    )(h, nw.reshape(1, D_MODEL), w)


def _outproj_kernel(h_ref, a0_ref, a1_ref, a2_ref, w0_ref, w1_ref, w2_ref, o_ref):
    acc = _dot(a0_ref[...], w0_ref[...])
    acc += _dot(a1_ref[...], w1_ref[...])
    acc += _dot(a2_ref[...], w2_ref[...])
    o_ref[...] = h_ref[...] + acc


def _outproj(h, a_hg, a_sb, a_rw, w0, w1, w2):
    t = h.shape[0]
    tm = _row_tile(t, 512)
    row = lambda i: (i, 0)
    fixed = lambda i: (0, 0)
    return pl.pallas_call(
        _outproj_kernel,
        grid=(t // tm,),
        in_specs=[
            pl.BlockSpec((tm, D_MODEL), row),
            pl.BlockSpec((tm, HG_WIDTH), row),
            pl.BlockSpec((tm, SB_WIDTH), row),
            pl.BlockSpec((tm, RW_WIDTH), row),
            pl.BlockSpec((HG_WIDTH, D_MODEL), fixed),
            pl.BlockSpec((SB_WIDTH, D_MODEL), fixed),
            pl.BlockSpec((RW_WIDTH, D_MODEL), fixed),
        ],
        out_specs=pl.BlockSpec((tm, D_MODEL), row),
        out_shape=jax.ShapeDtypeStruct((t, D_MODEL), F32),
        compiler_params=_cparams("parallel"),
        name="outproj",
    )(h, a_hg, a_sb, a_rw, w0, w1, w2)


HG_SUB = 16


def _hgrn2_kernel(q_ref, f_ref, i_ref, g_ref, lbp_ref, nw_ref, o_ref, s_ref, *, layer, n_chunks):
    @pl.when(pl.program_id(1) == 0)
    def _():
        s_ref[...] = jnp.zeros_like(s_ref)

    if layer > 0:
        p = lbp_ref[...]
        e = jnp.exp(p - jnp.max(p, axis=0, keepdims=True))
        sm = e / jnp.sum(e, axis=0, keepdims=True)
        lb_all = sm[1:2, :]
        for r in range(2, layer + 1):
            lb_all = lb_all + sm[r:r + 1, :]
        log_lb_all = jnp.log(lb_all)
        log_1mlb_all = jnp.log1p(-lb_all)

    row = lax.broadcasted_iota(jnp.int32, (CHUNK, CHUNK), 0)
    col = lax.broadcasted_iota(jnp.int32, (CHUNK, CHUNK), 1)
    tri = (row >= col).astype(BF16)
    srow = lax.broadcasted_iota(jnp.int32, (HG_SUB, 1), 0)
    scol = lax.broadcasted_iota(jnp.int32, (HG_SUB, CHUNK), 1)
    nw = nw_ref[...]
    n_sub = CHUNK // HG_SUB

    def chunk(c, carry):
        r0 = pl.multiple_of(c * CHUNK, CHUNK)
        rows = pl.ds(r0, CHUNK)
        hs = range(HG_HEADS)
        heads = [slice(h * HG_KEY, (h + 1) * HG_KEY) for h in hs]
        qs, ks, vbs, Gs = [], [], [], []
        for cols in heads:
            z = f_ref[rows, cols]
            ls = _log_sigmoid(z)
            if layer > 0:
                a = log_lb_all[:, cols]
                cc = log_1mlb_all[:, cols] + ls
                log_f = jnp.maximum(a, cc) + jnp.log1p(jnp.exp(-jnp.abs(a - cc)))
                k = (1.0 - lb_all[:, cols]) * jax.nn.sigmoid(-z)
            else:
                log_f = ls
                k = jax.nn.sigmoid(-z)
            qv = q_ref[rows, cols]
            q = qv * jax.nn.sigmoid(qv)
            v = i_ref[rows, cols]
            f_hi, f_mid, f_lo = _split3(log_f)
            Gs.append(_dot(tri, f_hi) + _dot(tri, f_mid) + _dot(tri, f_lo))
            qs.append(q)
            ks.append(k)
            vbs.append(v.astype(BF16))

        o_st, offs, pks = [], [], []
        for h in hs:
            q, k, G = qs[h], ks[h], Gs[h]
            kb = k.astype(BF16)
            g_end = G[CHUNK - 1:CHUNK, :]
            st = s_ref[h]
            o_st.append(_dot_t((q * jnp.exp(G)).astype(BF16), st.astype(BF16)))
            kd_end = (k * jnp.exp(g_end - G)).astype(BF16)
            upd = lax.dot_general(vbs[h], kd_end, (((0,), (0,)), ((), ())), preferred_element_type=F32)
            s_ref[h] = st * jnp.exp(g_end) + upd
            off_h, pk_h = [], []
            for i in range(n_sub):
                sl = slice(i * HG_SUB, (i + 1) * HG_SUB)
                G_i = G[sl, :]
                q_i = q[sl, :]
                if i > 0:
                    g_b = G[i * HG_SUB - 1:i * HG_SUB, :]
                    qd = (q_i * jnp.exp(G_i - g_b)).astype(BF16)
                    kd = (k * jnp.exp(jnp.minimum(g_b - G, 0.0))).astype(BF16)
                    off_h.append(_dot_t(qd, kd))
                else:
                    off_h.append(None)
                p_all = []
                for s in range(HG_SUB):
                    d = jnp.where(srow >= s, G_i - G_i[s:s + 1, :], NEG_BIG)
                    p_all.append((q_i * jnp.exp(d)).astype(BF16))
                pk_h.append(_dot_t(jnp.concatenate(p_all, axis=0), kb))
            offs.append(off_h)
            pks.append(pk_h)

        outs = []
        for h in hs:
            att_rows = []
            for i in range(n_sub):
                att_i = jnp.zeros((HG_SUB, CHUNK), F32)
                if i > 0:
                    att_i = jnp.where(scol < i * HG_SUB, offs[h][i], 0.0)
                for s in range(HG_SUB):
                    att_i = att_i + jnp.where(scol == i * HG_SUB + s,
                                              pks[h][i][s * HG_SUB:(s + 1) * HG_SUB], 0.0)
                att_rows.append(att_i)
            att = jnp.concatenate(att_rows, axis=0)
            outs.append(o_st[h] + _dot(att.astype(BF16), vbs[h]))

        for h, cols in enumerate(heads):
            gv = g_ref[rows, cols]
            o_ref[rows, cols] = (_rms(outs[h], nw) * (gv * jax.nn.sigmoid(gv))).astype(o_ref.dtype)
        return carry

    lax.fori_loop(0, n_chunks, chunk, 0)


def _hgrn2(proj, hg_lb, norm_w, layer, batch, seq):
    lb_rows = _row_tile(seq, 256)
    nblk = seq // lb_rows
    spec = lambda cb: pl.BlockSpec((lb_rows, HG_WIDTH), lambda b, i, cb=cb: (b * nblk + i, cb))
    return pl.pallas_call(
        functools.partial(_hgrn2_kernel, layer=layer, n_chunks=lb_rows // CHUNK),
        grid=(batch, nblk),
        in_specs=[spec(0), spec(1), spec(2), spec(3),
                  pl.BlockSpec((DEPTH, HG_WIDTH), lambda b, i: (0, 0)),
                  pl.BlockSpec((1, HG_KEY), lambda b, i: (0, 0))],
        out_specs=pl.BlockSpec((lb_rows, HG_WIDTH), lambda b, i: (b * nblk + i, 0)),
        out_shape=jax.ShapeDtypeStruct((batch * seq, HG_WIDTH), BF16),
        scratch_shapes=[pltpu.VMEM((HG_HEADS, HG_KEY, HG_KEY), F32)],
        compiler_params=_cparams("parallel", "arbitrary"),
        name="hgrn2",
    )(proj, proj, proj, proj, hg_lb, norm_w.reshape(1, HG_KEY))


def _sbprep_kernel(q_ref, k_ref, v_ref, qn_ref, kn_ref, qo_ref, ko_ref, vo_ref):
    qn = qn_ref[...]
    kn = kn_ref[...]
    for h in range(SB_HEADS):
        cols = slice(h * SB_HEAD_DIM, (h + 1) * SB_HEAD_DIM)
        qo_ref[:, cols] = (_rms(q_ref[:, cols], qn) * SB_SCALE).astype(BF16)
        ko_ref[:, cols] = _rms(k_ref[:, cols], kn).astype(BF16)
    vo_ref[...] = v_ref[...].astype(BF16)


def _sbprep(proj, qn_w, kn_w):
    t = proj.shape[0]
    tm = _row_tile(t, 512)
    spec = lambda cb: pl.BlockSpec((tm, SB_WIDTH), lambda i, cb=cb: (i, cb))
    wspec = pl.BlockSpec((1, SB_HEAD_DIM), lambda i: (0, 0))
    out = jax.ShapeDtypeStruct((t, SB_WIDTH), BF16)
    c0 = 4 * HG_WIDTH // SB_WIDTH
    return pl.pallas_call(
        _sbprep_kernel,
        grid=(t // tm,),
        in_specs=[spec(c0), spec(c0 + 1), spec(c0 + 2), wspec, wspec],
        out_specs=[spec(0), spec(0), spec(0)],
        out_shape=[out, out, out],
        compiler_params=_cparams("parallel"),
        name="sbprep",
    )(proj, proj, proj, qn_w.reshape(1, -1), kn_w.reshape(1, -1))


def _sbattn_kernel(q_ref, k_ref, v_ref, on_ref, o_ref, *, tq):
    i = pl.program_id(2)
    groups = tq // SB_BLOCK
    heads = [slice(h * SB_HEAD_DIM, (h + 1) * SB_HEAD_DIM) for h in range(SB_HPS)]
    qs = [q_ref[:, hc] for hc in heads]
    row = lax.broadcasted_iota(jnp.int32, (tq, SB_BLOCK), 0)
    col = lax.broadcasted_iota(jnp.int32, (tq, SB_BLOCK), 1)
    last = col == SB_BLOCK - 1
    mr = lax.broadcasted_iota(jnp.int32, (SB_BLOCK, SB_BLOCK), 0)
    mc = lax.broadcasted_iota(jnp.int32, (SB_BLOCK, SB_BLOCK), 1)
    m1 = ((mr > mc) | (mc == SB_BLOCK - 1)).astype(BF16)

    def slabs(kb, carry, diagonal):
        rows = pl.ds(pl.multiple_of(kb * tq, SB_BLOCK), tq)
        hs = range(SB_HPS)
        zs = [_dot_t(qs[h], k_ref[rows, heads[h]]) for h in hs]
        accs = [carry[h][0] for h in hs]
        cbs = [carry[h][1] for h in hs]
        parts = [[None] * groups for _ in hs]
        for u in range(groups - 1, -1, -1):
            if diagonal:
                keep = col + u * SB_BLOCK < row
            log_bs, rs = [], []
            for h in hs:
                zu = zs[h][:, u * SB_BLOCK:(u + 1) * SB_BLOCK]
                sp = jnp.log(1.0 + jnp.exp(-jnp.abs(zu)))
                log_b = jnp.minimum(zu, 0.0) - sp
                l1 = log_b - zu
                if diagonal:
                    l1 = jnp.where(keep, l1, 0.0)
                hi, lo = _split2(l1)
                log_bs.append(log_b)
                rs.append(_dot(hi, m1) + _dot(lo, m1))
            for h in hs:
                r = rs[h]
                a = jnp.exp(log_bs[h] + jnp.where(last, 0.0, r) + cbs[h])
                if diagonal:
                    a = jnp.where(keep, a, 0.0)
                parts[h][u] = a.astype(BF16)
                cbs[h] = cbs[h] + jnp.broadcast_to(r[:, SB_BLOCK - 1:SB_BLOCK], r.shape)
        for h in hs:
            accs[h] = accs[h] + _dot(jnp.concatenate(parts[h], axis=1), v_ref[rows, heads[h]])
        return tuple((accs[h], cbs[h]) for h in hs)

    zero = jnp.zeros((tq, SB_HEAD_DIM), F32)
    carry = slabs(i, ((zero, zero),) * SB_HPS, True)
    carry = lax.fori_loop(0, i, lambda jj, c: slabs(i - 1 - jj, c, False), carry)
    for h in range(SB_HPS):
        o_ref[:, heads[h]] = _rms(carry[h][0], on_ref[...]).astype(o_ref.dtype)


def _sbattn(q, k, v, on_w, batch, seq):
    tq = 3 * SB_BLOCK if seq % (3 * SB_BLOCK) == 0 else SB_BLOCK
    nq = seq // tq
    blk = pl.BlockSpec((tq, SB_HPS * SB_HEAD_DIM), lambda b, h, i: (b * nq + i, h))
    full = pl.BlockSpec((seq, SB_HPS * SB_HEAD_DIM), lambda b, h, i: (b, h))
    return pl.pallas_call(
        functools.partial(_sbattn_kernel, tq=tq),
        grid=(batch, SB_HEADS // SB_HPS, nq),
        in_specs=[blk, full, full, pl.BlockSpec((1, SB_HEAD_DIM), lambda b, h, i: (0, 0))],
        out_specs=blk,
        out_shape=jax.ShapeDtypeStruct((batch * seq, SB_WIDTH), BF16),
        compiler_params=_cparams("parallel", "parallel", "arbitrary"),
        name="sbattn",
    )(q, k, v, on_w.reshape(1, -1))


def _seg_matrix(n, seg):
    r = lax.broadcasted_iota(jnp.int32, (n, n), 0) // seg
    c = lax.broadcasted_iota(jnp.int32, (n, n), 1) // seg
    return (r == c).astype(BF16)


def _rwprep_kernel(*refs, layer):
    if layer > 0:
        (r_ref, k_ref, v_ref, lo_ref, mu_ref, w0_ref, w2_ref, a0_ref, a2_ref, g2_ref, kk_ref, ka_ref,
         v0_ref, v2_ref, vf_ref,
         ro_ref, wo_ref, ko_ref, vo_ref, kko_ref, bo_ref, go_ref, prev_ref) = refs
    else:
        (r_ref, k_ref, v_ref, lo_ref, mu_ref, w0_ref, w2_ref, a0_ref, a2_ref, g2_ref, kk_ref, ka_ref,
         ro_ref, wo_ref, ko_ref, vo_ref, kko_ref, bo_ref, go_ref, prev_ref) = refs

    @pl.when(pl.program_id(1) == 0)
    def _():
        prev_ref[...] = jnp.zeros_like(prev_ref)

    n = r_ref.shape[0]
    first = lax.broadcasted_iota(jnp.int32, (n, 1), 0) == 0

    def shifted(ref, part):
        p = ref[...]
        cols = slice(part * RW_WIDTH, (part + 1) * RW_WIDTH)
        prev = jnp.where(first, prev_ref[0:1, cols], pltpu.roll(p, 1, axis=0))
        prev_ref[0:1, cols] = p[n - 1:n, :]
        return p + (prev - p) * mu_ref[:, cols]

    r = shifted(r_ref, 0)
    k = shifted(k_ref, 1)
    v = shifted(v_ref, 2)
    lo = shifted(lo_ref, 3)
    lo_b = lo.astype(BF16)

    wl = -(w0_ref[...] + _dot(jnp.tanh(lo).astype(BF16), w2_ref[...]))
    w = -(jnp.maximum(wl, 0.0) + jnp.log1p(jnp.exp(-jnp.abs(wl)))) - 0.5
    decay = jnp.exp(-jnp.exp(w))
    a = jax.nn.sigmoid(a0_ref[...] + _dot(lo_b, a2_ref[...]))
    g = _dot(jax.nn.sigmoid(lo).astype(BF16), g2_ref[...])
    if layer > 0:
        v_first = jnp.concatenate([vf_ref[p] for p in range(RW_PAIRS)], axis=1)
        v = v + (v_first - v) * jax.nn.sigmoid(v0_ref[...] + _dot(lo_b, v2_ref[...]))

    seg = _seg_matrix(RW_WIDTH, RW_HEAD_DIM)
    kk = k * kk_ref[...]
    kk = kk * lax.rsqrt(jnp.maximum(_dot3(kk * kk, seg), 1e-24))
    k = k * (1.0 + (a - 1.0) * ka_ref[...])
    b = kk * a

    for p in range(RW_PAIRS):
        cols = slice(p * LANES, (p + 1) * LANES)
        ro_ref[p] = r[:, cols]
        wo_ref[p] = decay[:, cols]
        ko_ref[p] = k[:, cols]
        vo_ref[p] = v[:, cols]
        kko_ref[p] = kk[:, cols]
        bo_ref[p] = b[:, cols]
        go_ref[p] = g[:, cols]


def _rwprep(proj, mu, w0, w2p, a0, a2p, g2p, k_k, k_a, v_res, layer, batch, seq):
    lb_rows = _row_tile(seq, 256)
    nblk = seq // lb_rows
    c0 = RW_START // RW_WIDTH
    spec = lambda cb: pl.BlockSpec((lb_rows, RW_WIDTH), lambda b, i, cb=cb: (b * nblk + i, cb))
    pspec = pl.BlockSpec((RW_PAIRS, lb_rows, LANES), lambda b, i: (0, b * nblk + i, 0))
    vec = pl.BlockSpec((1, RW_WIDTH), lambda b, i: (0, 0))
    mat = pl.BlockSpec((RW_LORA_W, RW_WIDTH), lambda b, i: (0, 0))
    in_specs = [spec(c0), spec(c0 + 1), spec(c0 + 2), spec(c0 + 3),
                pl.BlockSpec((1, 4 * RW_WIDTH), lambda b, i: (0, 0)),
                vec, mat, vec, mat, mat, vec, vec]
    args = [proj, proj, proj, proj, mu, w0, w2p, a0, a2p, g2p, k_k, k_a]
    if layer > 0:
        v0, v2p, v_first = v_res
        in_specs += [vec, mat, pspec]
        args += [v0, v2p, v_first]
    out = jax.ShapeDtypeStruct((RW_PAIRS, batch * seq, LANES), F32)
    return pl.pallas_call(
        functools.partial(_rwprep_kernel, layer=layer),
        grid=(batch, nblk),
        in_specs=in_specs,
        out_specs=[pspec] * 7,
        out_shape=[out] * 7,
        scratch_shapes=[pltpu.VMEM((8, 4 * RW_WIDTH), F32)],
        compiler_params=_cparams("parallel", "arbitrary"),
        name="rwprep",
    )(*args)


def _rwscan_kernel(r_ref, w_ref, k_ref, v_ref, kk_ref, b_ref, y_ref, s_ref, *, batch, steps):
    @pl.when(pl.program_id(0) == 0)
    def _():
        s_ref[...] = jnp.zeros_like(s_ref)

    seg = _seg_matrix(LANES, RW_HEAD_DIM)
    vi = lax.broadcasted_iota(jnp.int32, (RW_HEAD_DIM, LANES), 0)
    li = lax.broadcasted_iota(jnp.int32, (RW_HEAD_DIM, LANES), 1)
    diag = (li % RW_HEAD_DIM) == vi
    n = RW_HEAD_DIM

    chains = [(p, bi) for p in range(RW_PAIRS) for bi in range(batch)]

    def y_row(yb):
        return jnp.sum(jnp.where(diag, yb, 0.0), axis=0, keepdims=True)

    def step(t, carry):
        row = pl.ds(t, 1)
        prow = pl.ds(jnp.maximum(t - 1, 0), 1)
        states, res = [], []
        for c, (p, bi) in enumerate(chains):
            s = s_ref[c]
            p1 = (s * kk_ref[p, bi, row, :]).astype(BF16)
            a = jnp.where(diag, v_ref[p, bi, row, :], 0.0).astype(BF16)
            p3 = (s * r_ref[p, bi, prow, :]).astype(BF16)
            states.append(s)
            res.append(_dot(jnp.concatenate([p1, a, p3], axis=0), seg))
        for c, (p, bi) in enumerate(chains):
            rc = res[c]
            sa = rc[0:n]
            vb = rc[n:2 * n]
            y_ref[p, bi, prow, :] = y_row(rc[2 * n:3 * n])
            s_ref[c] = (states[c] * w_ref[p, bi, row, :] - sa * b_ref[p, bi, row, :]
                        + vb * k_ref[p, bi, row, :])
        return carry

    def steps_unrolled(it, carry):
        for u in range(RW_UNROLL):
            step(it * RW_UNROLL + u, carry)
        return carry

    lax.fori_loop(0, steps // RW_UNROLL, steps_unrolled, 0)

    last = pl.ds(steps - 1, 1)
    for c, (p, bi) in enumerate(chains):
        yb = _dot((s_ref[c] * r_ref[p, bi, last, :]).astype(BF16), seg)
        y_ref[p, bi, last, :] = y_row(yb)


def _rwscan(r, w, k, v, kk, b, batch, seq):
    tb = _row_tile(seq, 64)
    spec = pl.BlockSpec((RW_PAIRS, batch, tb, LANES), lambda i: (0, 0, i, 0))
    shp = lambda x: x.reshape(RW_PAIRS, batch, seq, LANES)
    y = pl.pallas_call(
        functools.partial(_rwscan_kernel, batch=batch, steps=tb),
        grid=(seq // tb,),
        in_specs=[spec] * 6,
        out_specs=spec,
        out_shape=jax.ShapeDtypeStruct((RW_PAIRS, batch, seq, LANES), F32),
        scratch_shapes=[pltpu.VMEM((batch * RW_PAIRS, RW_HEAD_DIM, LANES), F32)],
        compiler_params=_cparams("arbitrary"),
        name="rwscan",
    )(shp(r), shp(w), shp(k), shp(v), shp(kk), shp(b))
    return y.reshape(RW_PAIRS, batch * seq, LANES)


def _rwpost_kernel(y_ref, r_ref, k_ref, v_ref, g_ref, lnw_ref, lnb_ref, rk_ref, o_ref):
    seg = _seg_matrix(LANES, RW_HEAD_DIM)
    y = y_ref[...]
    mu = _dot3(y, seg) * (1.0 / RW_HEAD_DIM)
    d = y - mu
    var = _dot3(d * d, seg) * (1.0 / RW_HEAD_DIM)
    yn = d * lax.rsqrt(var + RW_LN_EPS) * lnw_ref[...] + lnb_ref[...]
    bonus = _dot3(r_ref[...] * k_ref[...] * rk_ref[...], seg) * v_ref[...]
    o_ref[...] = ((yn + bonus) * g_ref[...]).astype(o_ref.dtype)


def _rwpost(y, r, k, v, g, ln_w, ln_b, r_k):
    t = y.shape[1]
    tm = _row_tile(t, 1024)
    row = pl.BlockSpec((None, tm, LANES), lambda p, i: (p, i, 0))
    vec = pl.BlockSpec((1, LANES), lambda p, i: (0, p))
    return pl.pallas_call(
        _rwpost_kernel,
        grid=(RW_PAIRS, t // tm),
        in_specs=[row] * 5 + [vec] * 3,
        out_specs=pl.BlockSpec((tm, LANES), lambda p, i: (i, p)),
        out_shape=jax.ShapeDtypeStruct((t, RW_WIDTH), BF16),
        compiler_params=_cparams("parallel", "parallel"),
        name="rwpost",
    )(y, r, k, v, g, ln_w, ln_b, r_k)


def _pad_rows(w, start):
    return jnp.pad(w, ((start, RW_LORA_W - start - w.shape[0]), (0, 0))).astype(BF16)


def _trunk(h, batch, seq, p, layers):
    vec = lambda x: x.reshape(1, -1).astype(F32)
    v_first = None
    for l in layers:
        h = _ffn(h, p["norm_ffn1"][l], p["ffn1_wi"][l].astype(BF16), p["ffn1_wo"][l].astype(BF16))

        if l == 0:
            w_l = jnp.pad(p["w_in"][l], ((0, 0), (0, RW_MV_LORA)))
            mu = jnp.pad(p["rw_mu"][l], (0, RW_MV_LORA))
        else:
            w_l = jnp.concatenate([p["w_in"][l], p["w_in_v"][l - 1]], axis=1)
            mu = jnp.concatenate([p["rw_mu"][l], p["rw_mu_v"][l - 1]])
        proj = _inproj(h, p["norm_mix"][l], w_l.astype(BF16))

        o_hg = _hgrn2(proj, p["hg_lb"].astype(F32), p["hg_norm"][l], l, batch, seq)

        q, k, v = _sbprep(proj, p["sb_qn"][l], p["sb_kn"][l])
        o_sb = _sbattn(q, k, v, p["sb_on"][l], batch, seq)

        v_res = None
        if l > 0:
            v_res = (vec(p["rw_v0"][l - 1]),
                     _pad_rows(p["rw_v2"][l - 1], RW_DECAY_LORA + RW_AAA_LORA + RW_GATE_LORA), v_first)
        r, w, k_r, v_r, kk, b, g = _rwprep(
            proj, vec(mu), vec(p["rw_w0"][l]), _pad_rows(p["rw_w2"][l], 0),
            vec(p["rw_a0"][l]), _pad_rows(p["rw_a2"][l], RW_DECAY_LORA),
            _pad_rows(p["rw_g2"][l], RW_DECAY_LORA + RW_AAA_LORA),
            vec(p["rw_kk"][l]), vec(p["rw_ka"][l]), v_res, l, batch, seq)
        if l == 0:
            v_first = v_r
        y = _rwscan(r, w, k_r, v_r, kk, b, batch, seq)
        o_rw = _rwpost(y, r, k_r, v_r, g, vec(p["rw_ln_w"][l]), vec(p["rw_ln_b"][l]), vec(p["rw_rk"][l]))

        w_out = p["w_out"][l].astype(BF16)
        h = _outproj(h, o_hg, o_sb, o_rw, w_out[:HG_WIDTH], w_out[HG_WIDTH:HG_WIDTH + SB_WIDTH],
                     w_out[HG_WIDTH + SB_WIDTH:])

        h = _ffn(h, p["norm_ffn2"][l], p["ffn2_wi"][l].astype(BF16), p["ffn2_wo"][l].astype(BF16))
    return h


def kernel(x, meta, norm_ffn1, ffn1_wi, ffn1_wo, norm_mix, w_in, w_in_v, hg_lb, hg_norm, sb_qn, sb_kn, sb_on, rw_mu, rw_mu_v, rw_w0, rw_w2, rw_a0, rw_a2, rw_g2, rw_v0, rw_v2, rw_kk, rw_ka, rw_rk, rw_ln_w, rw_ln_b, w_out, norm_ffn2, ffn2_wi, ffn2_wo):
    batch, s, d = x.shape
    l_real = N_META + s
    pad = (-l_real) % SB_BLOCK
    seq = l_real + pad
    h = jnp.concatenate([jnp.broadcast_to(meta.astype(x.dtype)[None], (batch, N_META, d)), x], axis=1)
    h = jnp.pad(h, ((0, 0), (0, pad), (0, 0))).reshape(batch * seq, d)
    p = dict(norm_ffn1=norm_ffn1, ffn1_wi=ffn1_wi, ffn1_wo=ffn1_wo, norm_mix=norm_mix, w_in=w_in,
             w_in_v=w_in_v, hg_lb=hg_lb, hg_norm=hg_norm, sb_qn=sb_qn, sb_kn=sb_kn, sb_on=sb_on,
             rw_mu=rw_mu, rw_mu_v=rw_mu_v, rw_w0=rw_w0, rw_w2=rw_w2, rw_a0=rw_a0, rw_a2=rw_a2,
             rw_g2=rw_g2, rw_v0=rw_v0, rw_v2=rw_v2, rw_kk=rw_kk, rw_ka=rw_ka, rw_rk=rw_rk,
             rw_ln_w=rw_ln_w, rw_ln_b=rw_ln_b, w_out=w_out, norm_ffn2=norm_ffn2, ffn2_wi=ffn2_wi,
             ffn2_wo=ffn2_wo)
    h = _trunk(h, batch, seq, p, range(DEPTH))
    return h.reshape(batch, seq, d)[:, N_META:l_real]
```

```python
import functools

import jax
import jax.numpy as jnp
from jax import lax
from jax.experimental import pallas as pl
from jax.experimental.pallas import tpu as pltpu

F32 = jnp.float32
BF16 = jnp.bfloat16

D_MODEL = 2048
DEPTH = 4
CHUNK = 64
N_META = 16
SB_BLOCK = 128
NORM_EPS = 1e-6
D_FF = 256 * ((8 * D_MODEL // 3 + 255) // 256)

HG_KEY = 128
HG_WIDTH = D_MODEL // 4
HG_HEADS = HG_WIDTH // HG_KEY

SB_HEAD_DIM = 128
SB_WIDTH = D_MODEL // 2
SB_HEADS = SB_WIDTH // SB_HEAD_DIM
SB_SCALE = SB_HEAD_DIM ** -0.5
SB_HPS = 2

RW_HEAD_DIM = 64
RW_WIDTH = D_MODEL - HG_WIDTH - SB_WIDTH
RW_HEADS = RW_WIDTH // RW_HEAD_DIM
RW_PAIRS = RW_HEADS // 2
RW_UNROLL = 4
RW_DECAY_LORA = max(32, int(round(1.8 * D_MODEL ** 0.5 / 32)) * 32)
RW_AAA_LORA = max(32, int(round(1.8 * D_MODEL ** 0.5 / 32)) * 32)
RW_MV_LORA = max(32, int(round(1.3 * D_MODEL ** 0.5 / 32)) * 32)
RW_GATE_LORA = max(32, int(round(0.6 * D_MODEL ** 0.8 / 32)) * 32)
RW_LN_EPS = 64e-5
RW_LORA_W = RW_DECAY_LORA + RW_AAA_LORA + RW_GATE_LORA + RW_MV_LORA

N_IN = 4 * HG_WIDTH + 3 * SB_WIDTH + 3 * RW_WIDTH + RW_DECAY_LORA + RW_AAA_LORA + RW_GATE_LORA
N_PROJ = N_IN + RW_MV_LORA
RW_START = 4 * HG_WIDTH + 3 * SB_WIDTH

LANES = 128
VMEM_LIMIT = 56 * 1024 * 1024
NEG_BIG = -1e30
LOG2E = 1.4426950408889634


def _cparams(*sem):
    return pltpu.CompilerParams(dimension_semantics=sem, vmem_limit_bytes=VMEM_LIMIT)


def _row_tile(t, want):
    tm = min(want, t)
    while t % tm:
        tm //= 2
    return tm


def _split2(x):
    hi = x.astype(BF16)
    lo = (x - hi.astype(F32)).astype(BF16)
    return hi, lo


def _split3(x):
    hi = x.astype(BF16)
    r = x - hi.astype(F32)
    mid = r.astype(BF16)
    lo = (r - mid.astype(F32)).astype(BF16)
    return hi, mid, lo


def _dot(a, b):
    return jnp.dot(a, b, preferred_element_type=F32)


def _dot_t(a, b):
    return lax.dot_general(a, b, (((1,), (1,)), ((), ())), preferred_element_type=F32)


def _dot3(x, m):
    hi, mid, lo = _split3(x)
    return _dot(hi, m) + _dot(mid, m) + _dot(lo, m)


def _log_sigmoid(z):
    return jnp.minimum(z, 0.0) - jnp.log1p(jnp.exp(-jnp.abs(z)))


def _rms(x, w):
    ms = jnp.mean(x * x, axis=-1, keepdims=True)
    return x * lax.rsqrt(ms + NORM_EPS) * w


def _ffn_kernel(x_ref, nw_ref, wg_ref, wu_ref, wo_ref, o_ref, xn_ref, acc_ref):
    j = pl.program_id(1)

    @pl.when(j == 0)
    def _():
        xn_ref[...] = _rms(x_ref[...], nw_ref[...]).astype(BF16)
        acc_ref[...] = jnp.zeros_like(acc_ref)

    xn = xn_ref[...]
    g = _dot(xn, wg_ref[...])
    u = _dot(xn, wu_ref[...])
    a = (g * jax.nn.sigmoid(g)) * u
    acc_ref[...] += _dot(a.astype(BF16), wo_ref[...])

    @pl.when(j == pl.num_programs(1) - 1)
    def _():
        o_ref[...] = x_ref[...] + 0.5 * acc_ref[...]


def _ffn(h, nw, wi, wo):
    t = h.shape[0]
    tm = _row_tile(t, 512)
    tf = 512
    nf = D_FF // tf
    return pl.pallas_call(
        _ffn_kernel,
        grid=(t // tm, nf),
        in_specs=[
            pl.BlockSpec((tm, D_MODEL), lambda i, j: (i, 0)),
            pl.BlockSpec((1, D_MODEL), lambda i, j: (0, 0)),
            pl.BlockSpec((D_MODEL, tf), lambda i, j: (0, j)),
            pl.BlockSpec((D_MODEL, tf), lambda i, j: (0, j + nf)),
            pl.BlockSpec((tf, D_MODEL), lambda i, j: (j, 0)),
        ],
        out_specs=pl.BlockSpec((tm, D_MODEL), lambda i, j: (i, 0)),
        out_shape=jax.ShapeDtypeStruct((t, D_MODEL), F32),
        scratch_shapes=[pltpu.VMEM((tm, D_MODEL), BF16), pltpu.VMEM((tm, D_MODEL), F32)],
        compiler_params=_cparams("parallel", "arbitrary"),
        name="ffn",
    )(h, nw.reshape(1, D_MODEL), wi, wi, wo)


def _inproj_kernel(x_ref, nw_ref, w_ref, o_ref, xn_ref):
    @pl.when(pl.program_id(1) == 0)
    def _():
        xn_ref[...] = _rms(x_ref[...], nw_ref[...]).astype(BF16)

    o_ref[...] = _dot(xn_ref[...], w_ref[...])


def _inproj(h, nw, w):
    t = h.shape[0]
    tm = 768 if t % 768 == 0 else _row_tile(t, 512)
    tn = N_PROJ // 4
    return pl.pallas_call(
        _inproj_kernel,
        grid=(t // tm, N_PROJ // tn),
        in_specs=[
            pl.BlockSpec((tm, D_MODEL), lambda i, j: (i, 0)),
            pl.BlockSpec((1, D_MODEL), lambda i, j: (0, 0)),
            pl.BlockSpec((D_MODEL, tn), lambda i, j: (0, j)),
        ],
        out_specs=pl.BlockSpec((tm, tn), lambda i, j: (i, j)),
        out_shape=jax.ShapeDtypeStruct((t, N_PROJ), F32),
        scratch_shapes=[pltpu.VMEM((tm, D_MODEL), BF16)],
        compiler_params=_cparams("parallel", "arbitrary"),
        name="inproj",
    )(h, nw.reshape(1, D_MODEL), w)


def _outproj_kernel(h_ref, a0_ref, a1_ref, a2_ref, w0_ref, w1_ref, w2_ref, o_ref):
    acc = _dot(a0_ref[...], w0_ref[...])
    acc += _dot(a1_ref[...], w1_ref[...])
    acc += _dot(a2_ref[...], w2_ref[...])
    o_ref[...] = h_ref[...] + acc


def _outproj(h, a_hg, a_sb, a_rw, w0, w1, w2):
    t = h.shape[0]
    tm = _row_tile(t, 512)
    row = lambda i: (i, 0)
    fixed = lambda i: (0, 0)
    return pl.pallas_call(
        _outproj_kernel,
        grid=(t // tm,),
        in_specs=[
            pl.BlockSpec((tm, D_MODEL), row),
            pl.BlockSpec((tm, HG_WIDTH), row),
            pl.BlockSpec((tm, SB_WIDTH), row),
            pl.BlockSpec((tm, RW_WIDTH), row),
            pl.BlockSpec((HG_WIDTH, D_MODEL), fixed),
            pl.BlockSpec((SB_WIDTH, D_MODEL), fixed),
            pl.BlockSpec((RW_WIDTH, D_MODEL), fixed),
        ],
        out_specs=pl.BlockSpec((tm, D_MODEL), row),
        out_shape=jax.ShapeDtypeStruct((t, D_MODEL), F32),
        compiler_params=_cparams("parallel"),
        name="outproj",
    )(h, a_hg, a_sb, a_rw, w0, w1, w2)


HG_SUB = 16


def _hgrn2_kernel(q_ref, f_ref, i_ref, g_ref, lbp_ref, nw_ref, o_ref, s_ref, *, layer, n_chunks):
    @pl.when(pl.program_id(1) == 0)
    def _():
        s_ref[...] = jnp.zeros_like(s_ref)

    if layer > 0:
        p = lbp_ref[...]
        e = jnp.exp(p - jnp.max(p, axis=0, keepdims=True))
        sm = e / jnp.sum(e, axis=0, keepdims=True)
        lb_all = sm[1:2, :]
        for r in range(2, layer + 1):
            lb_all = lb_all + sm[r:r + 1, :]
        log_lb_all = jnp.log(lb_all)
        log_1mlb_all = jnp.log1p(-lb_all)

    row = lax.broadcasted_iota(jnp.int32, (CHUNK, CHUNK), 0)
    col = lax.broadcasted_iota(jnp.int32, (CHUNK, CHUNK), 1)
    tri = (row >= col).astype(BF16)
    srow = lax.broadcasted_iota(jnp.int32, (HG_SUB, 1), 0)
    scol = lax.broadcasted_iota(jnp.int32, (HG_SUB, CHUNK), 1)
    nw = nw_ref[...]
    n_sub = CHUNK // HG_SUB

    def chunk(c, carry):
        r0 = pl.multiple_of(c * CHUNK, CHUNK)
        rows = pl.ds(r0, CHUNK)
        hs = range(HG_HEADS)
        heads = [slice(h * HG_KEY, (h + 1) * HG_KEY) for h in hs]
        qs, ks, vbs, Gs = [], [], [], []
        for cols in heads:
            z = f_ref[rows, cols]
            ls = _log_sigmoid(z)
            if layer > 0:
                a = log_lb_all[:, cols]
                cc = log_1mlb_all[:, cols] + ls
                log_f = jnp.maximum(a, cc) + jnp.log1p(jnp.exp(-jnp.abs(a - cc)))
                k = (1.0 - lb_all[:, cols]) * jax.nn.sigmoid(-z)
            else:
                log_f = ls
                k = jax.nn.sigmoid(-z)
            qv = q_ref[rows, cols]
            q = qv * jax.nn.sigmoid(qv)
            v = i_ref[rows, cols]
            f_hi, f_mid, f_lo = _split3(log_f)
            Gs.append(_dot(tri, f_hi) + _dot(tri, f_mid) + _dot(tri, f_lo))
            qs.append(q)
            ks.append(k)
            vbs.append(v.astype(BF16))

        o_st, offs, pks = [], [], []
        for h in hs:
            q, k, G = qs[h], ks[h], Gs[h]
            kb = k.astype(BF16)
            g_end = G[CHUNK - 1:CHUNK, :]
            st = s_ref[h]
            o_st.append(_dot_t((q * jnp.exp(G)).astype(BF16), st.astype(BF16)))
            kd_end = (k * jnp.exp(g_end - G)).astype(BF16)
            upd = lax.dot_general(vbs[h], kd_end, (((0,), (0,)), ((), ())), preferred_element_type=F32)
            s_ref[h] = st * jnp.exp(g_end) + upd
            off_h, pk_h = [], []
            for i in range(n_sub):
                sl = slice(i * HG_SUB, (i + 1) * HG_SUB)
                G_i = G[sl, :]
                q_i = q[sl, :]
                if i > 0:
                    g_b = G[i * HG_SUB - 1:i * HG_SUB, :]
                    qd = (q_i * jnp.exp(G_i - g_b)).astype(BF16)
                    kd = (k * jnp.exp(jnp.minimum(g_b - G, 0.0))).astype(BF16)
                    off_h.append(_dot_t(qd, kd))
                else:
                    off_h.append(None)
                p_all = []
                for s in range(HG_SUB):
                    d = jnp.where(srow >= s, G_i - G_i[s:s + 1, :], NEG_BIG)
                    p_all.append((q_i * jnp.exp(d)).astype(BF16))
                pk_h.append(_dot_t(jnp.concatenate(p_all, axis=0), kb))
            offs.append(off_h)
            pks.append(pk_h)

        outs = []
        for h in hs:
            att_rows = []
            for i in range(n_sub):
                att_i = jnp.zeros((HG_SUB, CHUNK), F32)
                if i > 0:
                    att_i = jnp.where(scol < i * HG_SUB, offs[h][i], 0.0)
                for s in range(HG_SUB):
                    att_i = att_i + jnp.where(scol == i * HG_SUB + s,
                                              pks[h][i][s * HG_SUB:(s + 1) * HG_SUB], 0.0)
                att_rows.append(att_i)
            att = jnp.concatenate(att_rows, axis=0)
            outs.append(o_st[h] + _dot(att.astype(BF16), vbs[h]))

        for h, cols in enumerate(heads):
            gv = g_ref[rows, cols]
            o_ref[rows, cols] = (_rms(outs[h], nw) * (gv * jax.nn.sigmoid(gv))).astype(o_ref.dtype)
        return carry

    lax.fori_loop(0, n_chunks, chunk, 0)


def _hgrn2(proj, hg_lb, norm_w, layer, batch, seq):
    lb_rows = _row_tile(seq, 256)
    nblk = seq // lb_rows
    spec = lambda cb: pl.BlockSpec((lb_rows, HG_WIDTH), lambda b, i, cb=cb: (b * nblk + i, cb))
    return pl.pallas_call(
        functools.partial(_hgrn2_kernel, layer=layer, n_chunks=lb_rows // CHUNK),
        grid=(batch, nblk),
        in_specs=[spec(0), spec(1), spec(2), spec(3),
                  pl.BlockSpec((DEPTH, HG_WIDTH), lambda b, i: (0, 0)),
                  pl.BlockSpec((1, HG_KEY), lambda b, i: (0, 0))],
        out_specs=pl.BlockSpec((lb_rows, HG_WIDTH), lambda b, i: (b * nblk + i, 0)),
        out_shape=jax.ShapeDtypeStruct((batch * seq, HG_WIDTH), BF16),
        scratch_shapes=[pltpu.VMEM((HG_HEADS, HG_KEY, HG_KEY), F32)],
        compiler_params=_cparams("parallel", "arbitrary"),
        name="hgrn2",
    )(proj, proj, proj, proj, hg_lb, norm_w.reshape(1, HG_KEY))


def _sbprep_kernel(q_ref, k_ref, v_ref, qn_ref, kn_ref, qo_ref, ko_ref, vo_ref):
    qn = qn_ref[...]
    kn = kn_ref[...]
    for h in range(SB_HEADS):
        cols = slice(h * SB_HEAD_DIM, (h + 1) * SB_HEAD_DIM)
        qo_ref[:, cols] = (_rms(q_ref[:, cols], qn) * SB_SCALE).astype(BF16)
        ko_ref[:, cols] = _rms(k_ref[:, cols], kn).astype(BF16)
    vo_ref[...] = v_ref[...].astype(BF16)


def _sbprep(proj, qn_w, kn_w):
    t = proj.shape[0]
    tm = _row_tile(t, 512)
    spec = lambda cb: pl.BlockSpec((tm, SB_WIDTH), lambda i, cb=cb: (i, cb))
    wspec = pl.BlockSpec((1, SB_HEAD_DIM), lambda i: (0, 0))
    out = jax.ShapeDtypeStruct((t, SB_WIDTH), BF16)
    c0 = 4 * HG_WIDTH // SB_WIDTH
    return pl.pallas_call(
        _sbprep_kernel,
        grid=(t // tm,),
        in_specs=[spec(c0), spec(c0 + 1), spec(c0 + 2), wspec, wspec],
        out_specs=[spec(0), spec(0), spec(0)],
        out_shape=[out, out, out],
        compiler_params=_cparams("parallel"),
        name="sbprep",
    )(proj, proj, proj, qn_w.reshape(1, -1), kn_w.reshape(1, -1))


def _sbattn_kernel(q_ref, k_ref, v_ref, on_ref, o_ref, z_ref, p_ref, acc_ref, cb_ref, *, tq):
    i = pl.program_id(2)
    groups = tq // SB_BLOCK
    heads = [slice(h * SB_HEAD_DIM, (h + 1) * SB_HEAD_DIM) for h in range(SB_HPS)]
    qs = [q_ref[:, hc] for hc in heads]
    row = lax.broadcasted_iota(jnp.int32, (tq, SB_BLOCK), 0)
    col = lax.broadcasted_iota(jnp.int32, (tq, SB_BLOCK), 1)
    mr = lax.broadcasted_iota(jnp.int32, (SB_BLOCK, SB_BLOCK), 0)
    mc = lax.broadcasted_iota(jnp.int32, (SB_BLOCK, SB_BLOCK), 1)
    m1 = jnp.where(mr >= mc, -1.0, 0.0).astype(BF16)

    hs = range(SB_HPS)

    def slab_rows(kb):
        return pl.ds(pl.multiple_of(kb * tq, SB_BLOCK), tq)

    def scores(kb, slot):
        rows = slab_rows(kb)
        for h in hs:
            z_ref[slot, h] = _dot_t(qs[h], k_ref[rows, heads[h]])

    def apply_v(kb):
        rows = slab_rows(kb)
        for h in hs:
            acc_ref[h] += _dot(p_ref[h], v_ref[rows, heads[h]])

    def weights(slot, diagonal):
        for u in range(groups - 1, -1, -1):
            gcols = slice(u * SB_BLOCK, (u + 1) * SB_BLOCK)
            qrows = slice(u * SB_BLOCK if diagonal else 0, tq)
            if diagonal:
                keep = (col + u * SB_BLOCK < row)[qrows]
            zus, rs = [], []
            for h in hs:
                zu = z_ref[slot, h, qrows, gcols]
                sp = jnp.maximum(zu, 0.0) + jnp.log(1.0 + jnp.exp2(jnp.abs(zu) * -LOG2E))
                if diagonal:
                    sp = jnp.where(keep, sp, 0.0)
                zus.append(zu)
                rs.append(_dot(sp.astype(BF16), m1))
            for h in hs:
                r = rs[h]
                cb = cb_ref[h, qrows, :]
                a = jnp.exp(zus[h] + r + cb)
                if diagonal:
                    a = jnp.where(keep, a, 0.0)
                    if u > 0:
                        p_ref[h, 0:u * SB_BLOCK, gcols] = jnp.zeros((u * SB_BLOCK, SB_BLOCK), BF16)
                p_ref[h, qrows, gcols] = a.astype(BF16)
                cb_ref[h, qrows, :] = cb + jnp.broadcast_to(r[:, 0:1], r.shape)

    acc_ref[...] = jnp.zeros_like(acc_ref)
    cb_ref[...] = jnp.zeros_like(cb_ref)
    scores(i, 0)
    scores(jnp.maximum(i - 1, 0), 1)
    weights(0, True)

    def trip(jj, cur):
        kb = i - 1 - jj
        apply_v(kb + 1)
        scores(jnp.maximum(kb - 1, 0), 1 - cur)
        weights(cur, False)

    def two_trips(m, carry):
        trip(2 * m, 1)
        trip(2 * m + 1, 0)
        return carry

    lax.fori_loop(0, i // 2, two_trips, 0)

    @pl.when(i % 2 == 1)
    def _():
        trip(i - 1, 1)

    apply_v(0)
    for h in hs:
        o_ref[:, heads[h]] = _rms(acc_ref[h], on_ref[...]).astype(o_ref.dtype)


def _sbattn(q, k, v, on_w, batch, seq):
    tq = 3 * SB_BLOCK if seq % (3 * SB_BLOCK) == 0 else SB_BLOCK
    nq = seq // tq
    blk = pl.BlockSpec((tq, SB_HPS * SB_HEAD_DIM), lambda b, h, i: (b * nq + i, h))
    full = pl.BlockSpec((seq, SB_HPS * SB_HEAD_DIM), lambda b, h, i: (b, h))
    return pl.pallas_call(
        functools.partial(_sbattn_kernel, tq=tq),
        grid=(batch, SB_HEADS // SB_HPS, nq),
        in_specs=[blk, full, full, pl.BlockSpec((1, SB_HEAD_DIM), lambda b, h, i: (0, 0))],
        out_specs=blk,
        out_shape=jax.ShapeDtypeStruct((batch * seq, SB_WIDTH), BF16),
        scratch_shapes=[pltpu.VMEM((2, SB_HPS, tq, tq), F32), pltpu.VMEM((SB_HPS, tq, tq), BF16),
                        pltpu.VMEM((SB_HPS, tq, SB_HEAD_DIM), F32), pltpu.VMEM((SB_HPS, tq, SB_HEAD_DIM), F32)],
        compiler_params=_cparams("parallel", "parallel", "arbitrary"),
        name="sbattn",
    )(q, k, v, on_w.reshape(1, -1))


def _seg_matrix(n, seg):
    r = lax.broadcasted_iota(jnp.int32, (n, n), 0) // seg
    c = lax.broadcasted_iota(jnp.int32, (n, n), 1) // seg
    return (r == c).astype(BF16)


def _rwprep_kernel(*refs, layer):
    if layer > 0:
        (r_ref, k_ref, v_ref, lo_ref, mu_ref, w0_ref, w2_ref, a0_ref, a2_ref, g2_ref, kk_ref, ka_ref,
         v0_ref, v2_ref, vf_ref,
         ro_ref, wo_ref, ko_ref, vo_ref, kko_ref, bo_ref, go_ref, prev_ref) = refs
    else:
        (r_ref, k_ref, v_ref, lo_ref, mu_ref, w0_ref, w2_ref, a0_ref, a2_ref, g2_ref, kk_ref, ka_ref,
         ro_ref, wo_ref, ko_ref, vo_ref, kko_ref, bo_ref, go_ref, prev_ref) = refs

    @pl.when(pl.program_id(1) == 0)
    def _():
        prev_ref[...] = jnp.zeros_like(prev_ref)

    n = r_ref.shape[0]
    first = lax.broadcasted_iota(jnp.int32, (n, 1), 0) == 0

    def shifted(ref, part):
        p = ref[...]
        cols = slice(part * RW_WIDTH, (part + 1) * RW_WIDTH)
        prev = jnp.where(first, prev_ref[0:1, cols], pltpu.roll(p, 1, axis=0))
        prev_ref[0:1, cols] = p[n - 1:n, :]
        return p + (prev - p) * mu_ref[:, cols]

    r = shifted(r_ref, 0)
    k = shifted(k_ref, 1)
    v = shifted(v_ref, 2)
    lo = shifted(lo_ref, 3)
    lo_b = lo.astype(BF16)

    wl = -(w0_ref[...] + _dot(jnp.tanh(lo).astype(BF16), w2_ref[...]))
    w = -(jnp.maximum(wl, 0.0) + jnp.log1p(jnp.exp(-jnp.abs(wl)))) - 0.5
    decay = jnp.exp(-jnp.exp(w))
    a = jax.nn.sigmoid(a0_ref[...] + _dot(lo_b, a2_ref[...]))
    g = _dot(jax.nn.sigmoid(lo).astype(BF16), g2_ref[...])
    if layer > 0:
        v_first = jnp.concatenate([vf_ref[p] for p in range(RW_PAIRS)], axis=1)
        v = v + (v_first - v) * jax.nn.sigmoid(v0_ref[...] + _dot(lo_b, v2_ref[...]))

    seg = _seg_matrix(RW_WIDTH, RW_HEAD_DIM)
    kk = k * kk_ref[...]
    kk = kk * lax.rsqrt(jnp.maximum(_dot3(kk * kk, seg), 1e-24))
    k = k * (1.0 + (a - 1.0) * ka_ref[...])
    b = kk * a

    for p in range(RW_PAIRS):
        cols = slice(p * LANES, (p + 1) * LANES)
        ro_ref[p] = r[:, cols]
        wo_ref[p] = decay[:, cols]
        ko_ref[p] = k[:, cols]
        vo_ref[p] = v[:, cols]
        kko_ref[p] = kk[:, cols]
        bo_ref[p] = b[:, cols]
        go_ref[p] = g[:, cols]


def _rwprep(proj, mu, w0, w2p, a0, a2p, g2p, k_k, k_a, v_res, layer, batch, seq):
    lb_rows = _row_tile(seq, 256)
    nblk = seq // lb_rows
    c0 = RW_START // RW_WIDTH
    spec = lambda cb: pl.BlockSpec((lb_rows, RW_WIDTH), lambda b, i, cb=cb: (b * nblk + i, cb))
    pspec = pl.BlockSpec((RW_PAIRS, lb_rows, LANES), lambda b, i: (0, b * nblk + i, 0))
    vec = pl.BlockSpec((1, RW_WIDTH), lambda b, i: (0, 0))
    mat = pl.BlockSpec((RW_LORA_W, RW_WIDTH), lambda b, i: (0, 0))
    in_specs = [spec(c0), spec(c0 + 1), spec(c0 + 2), spec(c0 + 3),
                pl.BlockSpec((1, 4 * RW_WIDTH), lambda b, i: (0, 0)),
                vec, mat, vec, mat, mat, vec, vec]
    args = [proj, proj, proj, proj, mu, w0, w2p, a0, a2p, g2p, k_k, k_a]
    if layer > 0:
        v0, v2p, v_first = v_res
        in_specs += [vec, mat, pspec]
        args += [v0, v2p, v_first]
    out = jax.ShapeDtypeStruct((RW_PAIRS, batch * seq, LANES), F32)
    return pl.pallas_call(
        functools.partial(_rwprep_kernel, layer=layer),
        grid=(batch, nblk),
        in_specs=in_specs,
        out_specs=[pspec] * 7,
        out_shape=[out] * 7,
        scratch_shapes=[pltpu.VMEM((8, 4 * RW_WIDTH), F32)],
        compiler_params=_cparams("parallel", "arbitrary"),
        name="rwprep",
    )(*args)


def _rwscan_kernel(r_ref, w_ref, k_ref, v_ref, kk_ref, b_ref, y_ref, s_ref, *, batch, steps):
    @pl.when(pl.program_id(0) == 0)
    def _():
        s_ref[...] = jnp.zeros_like(s_ref)

    seg = _seg_matrix(LANES, RW_HEAD_DIM)
    vi = lax.broadcasted_iota(jnp.int32, (RW_HEAD_DIM, LANES), 0)
    li = lax.broadcasted_iota(jnp.int32, (RW_HEAD_DIM, LANES), 1)
    diag = (li % RW_HEAD_DIM) == vi
    n = RW_HEAD_DIM

    chains = [(p, bi) for p in range(RW_PAIRS) for bi in range(batch)]

    def y_row(yb):
        return jnp.sum(jnp.where(diag, yb, 0.0), axis=0, keepdims=True)

    def step(t, carry):
        row = pl.ds(t, 1)
        prow = pl.ds(jnp.maximum(t - 1, 0), 1)
        states, res = [], []
        for c, (p, bi) in enumerate(chains):
            s = s_ref[c]
            p1 = (s * kk_ref[p, bi, row, :]).astype(BF16)
            a = jnp.where(diag, v_ref[p, bi, row, :], 0.0).astype(BF16)
            p3 = (s * r_ref[p, bi, prow, :]).astype(BF16)
            states.append(s)
            res.append(_dot(jnp.concatenate([p1, a, p3], axis=0), seg))
        for c, (p, bi) in enumerate(chains):
            rc = res[c]
            sa = rc[0:n]
            vb = rc[n:2 * n]
            y_ref[p, bi, prow, :] = y_row(rc[2 * n:3 * n])
            s_ref[c] = (states[c] * w_ref[p, bi, row, :] - sa * b_ref[p, bi, row, :]
                        + vb * k_ref[p, bi, row, :])
        return carry

    def steps_unrolled(it, carry):
        for u in range(RW_UNROLL):
            step(it * RW_UNROLL + u, carry)
        return carry

    lax.fori_loop(0, steps // RW_UNROLL, steps_unrolled, 0)

    last = pl.ds(steps - 1, 1)
    for c, (p, bi) in enumerate(chains):
        yb = _dot((s_ref[c] * r_ref[p, bi, last, :]).astype(BF16), seg)
        y_ref[p, bi, last, :] = y_row(yb)


def _rwscan(r, w, k, v, kk, b, batch, seq):
    tb = _row_tile(seq, 64)
    spec = pl.BlockSpec((RW_PAIRS, batch, tb, LANES), lambda i: (0, 0, i, 0))
    shp = lambda x: x.reshape(RW_PAIRS, batch, seq, LANES)
    y = pl.pallas_call(
        functools.partial(_rwscan_kernel, batch=batch, steps=tb),
        grid=(seq // tb,),
        in_specs=[spec] * 6,
        out_specs=spec,
        out_shape=jax.ShapeDtypeStruct((RW_PAIRS, batch, seq, LANES), F32),
        scratch_shapes=[pltpu.VMEM((batch * RW_PAIRS, RW_HEAD_DIM, LANES), F32)],
        compiler_params=_cparams("arbitrary"),
        name="rwscan",
    )(shp(r), shp(w), shp(k), shp(v), shp(kk), shp(b))
    return y.reshape(RW_PAIRS, batch * seq, LANES)


def _rwpost_kernel(y_ref, r_ref, k_ref, v_ref, g_ref, lnw_ref, lnb_ref, rk_ref, o_ref):
    seg = _seg_matrix(LANES, RW_HEAD_DIM)
    y = y_ref[...]
    mu = _dot3(y, seg) * (1.0 / RW_HEAD_DIM)
    d = y - mu
    var = _dot3(d * d, seg) * (1.0 / RW_HEAD_DIM)
    yn = d * lax.rsqrt(var + RW_LN_EPS) * lnw_ref[...] + lnb_ref[...]
    bonus = _dot3(r_ref[...] * k_ref[...] * rk_ref[...], seg) * v_ref[...]
    o_ref[...] = ((yn + bonus) * g_ref[...]).astype(o_ref.dtype)


def _rwpost(y, r, k, v, g, ln_w, ln_b, r_k):
    t = y.shape[1]
    tm = _row_tile(t, 1024)
    row = pl.BlockSpec((None, tm, LANES), lambda p, i: (p, i, 0))
    vec = pl.BlockSpec((1, LANES), lambda p, i: (0, p))
    return pl.pallas_call(
        _rwpost_kernel,
        grid=(RW_PAIRS, t // tm),
        in_specs=[row] * 5 + [vec] * 3,
        out_specs=pl.BlockSpec((tm, LANES), lambda p, i: (i, p)),
        out_shape=jax.ShapeDtypeStruct((t, RW_WIDTH), BF16),
        compiler_params=_cparams("parallel", "parallel"),
        name="rwpost",
    )(y, r, k, v, g, ln_w, ln_b, r_k)


def _pad_rows(w, start):
    return jnp.pad(w, ((start, RW_LORA_W - start - w.shape[0]), (0, 0))).astype(BF16)


def _trunk(h, batch, seq, p, layers):
    vec = lambda x: x.reshape(1, -1).astype(F32)
    v_first = None
    for l in layers:
        h = _ffn(h, p["norm_ffn1"][l], p["ffn1_wi"][l].astype(BF16), p["ffn1_wo"][l].astype(BF16))

        if l == 0:
            w_l = jnp.pad(p["w_in"][l], ((0, 0), (0, RW_MV_LORA)))
            mu = jnp.pad(p["rw_mu"][l], (0, RW_MV_LORA))
        else:
            w_l = jnp.concatenate([p["w_in"][l], p["w_in_v"][l - 1]], axis=1)
            mu = jnp.concatenate([p["rw_mu"][l], p["rw_mu_v"][l - 1]])
        proj = _inproj(h, p["norm_mix"][l], w_l.astype(BF16))

        o_hg = _hgrn2(proj, p["hg_lb"].astype(F32), p["hg_norm"][l], l, batch, seq)

        q, k, v = _sbprep(proj, p["sb_qn"][l], p["sb_kn"][l])
        o_sb = _sbattn(q, k, v, p["sb_on"][l], batch, seq)

        v_res = None
        if l > 0:
            v_res = (vec(p["rw_v0"][l - 1]),
                     _pad_rows(p["rw_v2"][l - 1], RW_DECAY_LORA + RW_AAA_LORA + RW_GATE_LORA), v_first)
        r, w, k_r, v_r, kk, b, g = _rwprep(
            proj, vec(mu), vec(p["rw_w0"][l]), _pad_rows(p["rw_w2"][l], 0),
            vec(p["rw_a0"][l]), _pad_rows(p["rw_a2"][l], RW_DECAY_LORA),
            _pad_rows(p["rw_g2"][l], RW_DECAY_LORA + RW_AAA_LORA),
            vec(p["rw_kk"][l]), vec(p["rw_ka"][l]), v_res, l, batch, seq)
        if l == 0:
            v_first = v_r
        y = _rwscan(r, w, k_r, v_r, kk, b, batch, seq)
        o_rw = _rwpost(y, r, k_r, v_r, g, vec(p["rw_ln_w"][l]), vec(p["rw_ln_b"][l]), vec(p["rw_rk"][l]))

        w_out = p["w_out"][l].astype(BF16)
        h = _outproj(h, o_hg, o_sb, o_rw, w_out[:HG_WIDTH], w_out[HG_WIDTH:HG_WIDTH + SB_WIDTH],
                     w_out[HG_WIDTH + SB_WIDTH:])

        h = _ffn(h, p["norm_ffn2"][l], p["ffn2_wi"][l].astype(BF16), p["ffn2_wo"][l].astype(BF16))
    return h


def kernel(x, meta, norm_ffn1, ffn1_wi, ffn1_wo, norm_mix, w_in, w_in_v, hg_lb, hg_norm, sb_qn, sb_kn, sb_on, rw_mu, rw_mu_v, rw_w0, rw_w2, rw_a0, rw_a2, rw_g2, rw_v0, rw_v2, rw_kk, rw_ka, rw_rk, rw_ln_w, rw_ln_b, w_out, norm_ffn2, ffn2_wi, ffn2_wo):
    batch, s, d = x.shape
    l_real = N_META + s
    pad = (-l_real) % SB_BLOCK
    seq = l_real + pad
    h = jnp.concatenate([jnp.broadcast_to(meta.astype(x.dtype)[None], (batch, N_META, d)), x], axis=1)
    h = jnp.pad(h, ((0, 0), (0, pad), (0, 0))).reshape(batch * seq, d)
    p = dict(norm_ffn1=norm_ffn1, ffn1_wi=ffn1_wi, ffn1_wo=ffn1_wo, norm_mix=norm_mix, w_in=w_in,
             w_in_v=w_in_v, hg_lb=hg_lb, hg_norm=hg_norm, sb_qn=sb_qn, sb_kn=sb_kn, sb_on=sb_on,
             rw_mu=rw_mu, rw_mu_v=rw_mu_v, rw_w0=rw_w0, rw_w2=rw_w2, rw_a0=rw_a0, rw_a2=rw_a2,
             rw_g2=rw_g2, rw_v0=rw_v0, rw_v2=rw_v2, rw_kk=rw_kk, rw_ka=rw_ka, rw_rk=rw_rk,
             rw_ln_w=rw_ln_w, rw_ln_b=rw_ln_b, w_out=w_out, norm_ffn2=norm_ffn2, ffn2_wi=ffn2_wi,
             ffn2_wo=ffn2_wo)
    h = _trunk(h, batch, seq, p, range(DEPTH))
    return h.reshape(batch, seq, d)[:, N_META:l_real]
```

```python
import functools

import jax
import jax.numpy as jnp
from jax import lax
from jax.experimental import pallas as pl
from jax.experimental.pallas import tpu as pltpu

F32 = jnp.float32
BF16 = jnp.bfloat16

D_MODEL = 2048
DEPTH = 4
CHUNK = 64
N_META = 16
SB_BLOCK = 128
NORM_EPS = 1e-6
D_FF = 256 * ((8 * D_MODEL // 3 + 255) // 256)

HG_KEY = 128
HG_WIDTH = D_MODEL // 4
HG_HEADS = HG_WIDTH // HG_KEY

SB_HEAD_DIM = 128
SB_WIDTH = D_MODEL // 2
SB_HEADS = SB_WIDTH // SB_HEAD_DIM
SB_SCALE = SB_HEAD_DIM ** -0.5
SB_HPS = 2

RW_HEAD_DIM = 64
RW_WIDTH = D_MODEL - HG_WIDTH - SB_WIDTH
RW_HEADS = RW_WIDTH // RW_HEAD_DIM
RW_PAIRS = RW_HEADS // 2
RW_UNROLL = 4
RW_DECAY_LORA = max(32, int(round(1.8 * D_MODEL ** 0.5 / 32)) * 32)
RW_AAA_LORA = max(32, int(round(1.8 * D_MODEL ** 0.5 / 32)) * 32)
RW_MV_LORA = max(32, int(round(1.3 * D_MODEL ** 0.5 / 32)) * 32)
RW_GATE_LORA = max(32, int(round(0.6 * D_MODEL ** 0.8 / 32)) * 32)
RW_LN_EPS = 64e-5
RW_LORA_W = RW_DECAY_LORA + RW_AAA_LORA + RW_GATE_LORA + RW_MV_LORA

N_IN = 4 * HG_WIDTH + 3 * SB_WIDTH + 3 * RW_WIDTH + RW_DECAY_LORA + RW_AAA_LORA + RW_GATE_LORA
N_PROJ = N_IN + RW_MV_LORA
RW_START = 4 * HG_WIDTH + 3 * SB_WIDTH

LANES = 128
VMEM_LIMIT = 56 * 1024 * 1024
NEG_BIG = -1e30
LOG2E = 1.4426950408889634


def _cparams(*sem):
    return pltpu.CompilerParams(dimension_semantics=sem, vmem_limit_bytes=VMEM_LIMIT)


def _row_tile(t, want):
    tm = min(want, t)
    while t % tm:
        tm //= 2
    return tm


def _split2(x):
    hi = x.astype(BF16)
    lo = (x - hi.astype(F32)).astype(BF16)
    return hi, lo


def _split3(x):
    hi = x.astype(BF16)
    r = x - hi.astype(F32)
    mid = r.astype(BF16)
    lo = (r - mid.astype(F32)).astype(BF16)
    return hi, mid, lo


def _dot(a, b):
    return jnp.dot(a, b, preferred_element_type=F32)


def _dot_t(a, b):
    return lax.dot_general(a, b, (((1,), (1,)), ((), ())), preferred_element_type=F32)


def _dot3(x, m):
    hi, mid, lo = _split3(x)
    return _dot(hi, m) + _dot(mid, m) + _dot(lo, m)


def _log_sigmoid(z):
    return jnp.minimum(z, 0.0) - jnp.log1p(jnp.exp(-jnp.abs(z)))


def _rms(x, w):
    ms = jnp.mean(x * x, axis=-1, keepdims=True)
    return x * lax.rsqrt(ms + NORM_EPS) * w


def _ffn_kernel(x_ref, nw_ref, wg_ref, wu_ref, wo_ref, o_ref, xn_ref, acc_ref):
    j = pl.program_id(1)

    @pl.when(j == 0)
    def _():
        xn_ref[...] = _rms(x_ref[...], nw_ref[...]).astype(BF16)
        acc_ref[...] = jnp.zeros_like(acc_ref)

    xn = xn_ref[...]
    g = _dot(xn, wg_ref[...])
    u = _dot(xn, wu_ref[...])
    a = (g * jax.nn.sigmoid(g)) * u
    acc_ref[...] += _dot(a.astype(BF16), wo_ref[...])

    @pl.when(j == pl.num_programs(1) - 1)
    def _():
        o_ref[...] = x_ref[...] + 0.5 * acc_ref[...]


def _ffn(h, nw, wi, wo, layer):
    t = h.shape[0]
    tm = _row_tile(t, 512)
    tf = 512
    nf = D_FF // tf
    return pl.pallas_call(
        _ffn_kernel,
        grid=(t // tm, nf),
        in_specs=[
            pl.BlockSpec((tm, D_MODEL), lambda i, j: (i, 0)),
            pl.BlockSpec((1, D_MODEL), lambda i, j: (0, 0)),
            pl.BlockSpec((None, D_MODEL, tf), lambda i, j: (layer, 0, j)),
            pl.BlockSpec((None, D_MODEL, tf), lambda i, j: (layer, 0, j + nf)),
            pl.BlockSpec((None, tf, D_MODEL), lambda i, j: (layer, j, 0)),
        ],
        out_specs=pl.BlockSpec((tm, D_MODEL), lambda i, j: (i, 0)),
        out_shape=jax.ShapeDtypeStruct((t, D_MODEL), F32),
        scratch_shapes=[pltpu.VMEM((tm, D_MODEL), BF16), pltpu.VMEM((tm, D_MODEL), F32)],
        compiler_params=_cparams("parallel", "arbitrary"),
        name="ffn",
    )(h, nw.reshape(1, D_MODEL), wi, wi, wo)


def _inproj_kernel(x_ref, nw_ref, w_ref, o_ref, xn_ref):
    @pl.when(pl.program_id(1) == 0)
    def _():
        xn_ref[...] = _rms(x_ref[...], nw_ref[...]).astype(BF16)

    o_ref[...] = _dot(xn_ref[...], w_ref[...])


def _inproj(h, nw, w, layer):
    t = h.shape[0]
    tm = 768 if t % 768 == 0 else _row_tile(t, 512)
    tn = N_PROJ // 4
    return pl.pallas_call(
        _inproj_kernel,
        grid=(t // tm, N_PROJ // tn),
        in_specs=[
            pl.BlockSpec((tm, D_MODEL), lambda i, j: (i, 0)),
            pl.BlockSpec((1, D_MODEL), lambda i, j: (0, 0)),
            pl.BlockSpec((None, D_MODEL, tn), lambda i, j: (layer, 0, j)),
        ],
        out_specs=pl.BlockSpec((tm, tn), lambda i, j: (i, j)),
        out_shape=jax.ShapeDtypeStruct((t, N_PROJ), F32),
        scratch_shapes=[pltpu.VMEM((tm, D_MODEL), BF16)],
        compiler_params=_cparams("parallel", "arbitrary"),
        name="inproj",
    )(h, nw.reshape(1, D_MODEL), w)


def _outproj_kernel(h_ref, a0_ref, a1_ref, a2_ref, w_ref, o_ref):
    acc = _dot(a0_ref[...], w_ref[0:HG_WIDTH, :])
    acc += _dot(a1_ref[...], w_ref[HG_WIDTH:HG_WIDTH + SB_WIDTH, :])
    acc += _dot(a2_ref[...], w_ref[HG_WIDTH + SB_WIDTH:, :])
    o_ref[...] = h_ref[...] + acc


def _outproj(h, a_hg, a_sb, a_rw, w_out, layer):
    t = h.shape[0]
    tm = _row_tile(t, 512)
    row = lambda i: (i, 0)
    return pl.pallas_call(
        _outproj_kernel,
        grid=(t // tm,),
        in_specs=[
            pl.BlockSpec((tm, D_MODEL), row),
            pl.BlockSpec((tm, HG_WIDTH), row),
            pl.BlockSpec((tm, SB_WIDTH), row),
            pl.BlockSpec((tm, RW_WIDTH), row),
            pl.BlockSpec((None, D_MODEL, D_MODEL), lambda i: (layer, 0, 0)),
        ],
        out_specs=pl.BlockSpec((tm, D_MODEL), row),
        out_shape=jax.ShapeDtypeStruct((t, D_MODEL), F32),
        compiler_params=_cparams("parallel"),
        name="outproj",
    )(h, a_hg, a_sb, a_rw, w_out)


HG_SUB = 16


def _hgrn2_kernel(q_ref, f_ref, i_ref, g_ref, lbp_ref, nw_ref, o_ref, s_ref, *, layer, n_chunks):
    @pl.when(pl.program_id(1) == 0)
    def _():
        s_ref[...] = jnp.zeros_like(s_ref)

    if layer > 0:
        p = lbp_ref[...]
        e = jnp.exp(p - jnp.max(p, axis=0, keepdims=True))
        sm = e / jnp.sum(e, axis=0, keepdims=True)
        lb_all = sm[1:2, :]
        for r in range(2, layer + 1):
            lb_all = lb_all + sm[r:r + 1, :]
        log_lb_all = jnp.log(lb_all)
        log_1mlb_all = jnp.log1p(-lb_all)

    row = lax.broadcasted_iota(jnp.int32, (CHUNK, CHUNK), 0)
    col = lax.broadcasted_iota(jnp.int32, (CHUNK, CHUNK), 1)
    tri = (row >= col).astype(BF16)
    srow = lax.broadcasted_iota(jnp.int32, (HG_SUB, 1), 0)
    scol = lax.broadcasted_iota(jnp.int32, (HG_SUB, CHUNK), 1)
    nw = nw_ref[...]
    n_sub = CHUNK // HG_SUB

    def chunk(c, carry):
        r0 = pl.multiple_of(c * CHUNK, CHUNK)
        rows = pl.ds(r0, CHUNK)
        hs = range(HG_HEADS)
        heads = [slice(h * HG_KEY, (h + 1) * HG_KEY) for h in hs]
        qs, ks, vbs, Gs = [], [], [], []
        for cols in heads:
            z = f_ref[rows, cols]
            ls = _log_sigmoid(z)
            if layer > 0:
                a = log_lb_all[:, cols]
                cc = log_1mlb_all[:, cols] + ls
                log_f = jnp.maximum(a, cc) + jnp.log1p(jnp.exp(-jnp.abs(a - cc)))
                k = (1.0 - lb_all[:, cols]) * jax.nn.sigmoid(-z)
            else:
                log_f = ls
                k = jax.nn.sigmoid(-z)
            qv = q_ref[rows, cols]
            q = qv * jax.nn.sigmoid(qv)
            v = i_ref[rows, cols]
            f_hi, f_mid, f_lo = _split3(log_f)
            Gs.append(_dot(tri, f_hi) + _dot(tri, f_mid) + _dot(tri, f_lo))
            qs.append(q)
            ks.append(k)
            vbs.append(v.astype(BF16))

        o_st, offs, pks = [], [], []
        for h in hs:
            q, k, G = qs[h], ks[h], Gs[h]
            kb = k.astype(BF16)
            g_end = G[CHUNK - 1:CHUNK, :]
            st = s_ref[h]
            o_st.append(_dot_t((q * jnp.exp(G)).astype(BF16), st.astype(BF16)))
            kd_end = (k * jnp.exp(g_end - G)).astype(BF16)
            upd = lax.dot_general(vbs[h], kd_end, (((0,), (0,)), ((), ())), preferred_element_type=F32)
            s_ref[h] = st * jnp.exp(g_end) + upd
            off_h, pk_h = [], []
            for i in range(n_sub):
                sl = slice(i * HG_SUB, (i + 1) * HG_SUB)
                G_i = G[sl, :]
                q_i = q[sl, :]
                if i > 0:
                    g_b = G[i * HG_SUB - 1:i * HG_SUB, :]
                    qd = (q_i * jnp.exp(G_i - g_b)).astype(BF16)
                    kd = (k * jnp.exp(jnp.minimum(g_b - G, 0.0))).astype(BF16)
                    off_h.append(_dot_t(qd, kd))
                else:
                    off_h.append(None)
                p_all = []
                for s in range(HG_SUB):
                    d = jnp.where(srow >= s, G_i - G_i[s:s + 1, :], NEG_BIG)
                    p_all.append((q_i * jnp.exp(d)).astype(BF16))
                pk_h.append(_dot_t(jnp.concatenate(p_all, axis=0), kb))
            offs.append(off_h)
            pks.append(pk_h)

        outs = []
        for h in hs:
            att_rows = []
            for i in range(n_sub):
                att_i = jnp.zeros((HG_SUB, CHUNK), F32)
                if i > 0:
                    att_i = jnp.where(scol < i * HG_SUB, offs[h][i], 0.0)
                for s in range(HG_SUB):
                    att_i = att_i + jnp.where(scol == i * HG_SUB + s,
                                              pks[h][i][s * HG_SUB:(s + 1) * HG_SUB], 0.0)
                att_rows.append(att_i)
            att = jnp.concatenate(att_rows, axis=0)
            outs.append(o_st[h] + _dot(att.astype(BF16), vbs[h]))

        for h, cols in enumerate(heads):
            gv = g_ref[rows, cols]
            o_ref[rows, cols] = (_rms(outs[h], nw) * (gv * jax.nn.sigmoid(gv))).astype(o_ref.dtype)
        return carry

    lax.fori_loop(0, n_chunks, chunk, 0)


def _hgrn2(proj, hg_lb, norm_w, layer, batch, seq):
    lb_rows = 6 * CHUNK if seq % (6 * CHUNK) == 0 else _row_tile(seq, 256)
    nblk = seq // lb_rows
    spec = lambda cb: pl.BlockSpec((lb_rows, HG_WIDTH), lambda b, i, cb=cb: (b * nblk + i, cb))
    return pl.pallas_call(
        functools.partial(_hgrn2_kernel, layer=layer, n_chunks=lb_rows // CHUNK),
        grid=(batch, nblk),
        in_specs=[spec(0), spec(1), spec(2), spec(3),
                  pl.BlockSpec((DEPTH, HG_WIDTH), lambda b, i: (0, 0)),
                  pl.BlockSpec((1, HG_KEY), lambda b, i: (0, 0))],
        out_specs=pl.BlockSpec((lb_rows, HG_WIDTH), lambda b, i: (b * nblk + i, 0)),
        out_shape=jax.ShapeDtypeStruct((batch * seq, HG_WIDTH), BF16),
        scratch_shapes=[pltpu.VMEM((HG_HEADS, HG_KEY, HG_KEY), F32)],
        compiler_params=_cparams("parallel", "arbitrary"),
        name="hgrn2",
    )(proj, proj, proj, proj, hg_lb, norm_w.reshape(1, HG_KEY))


def _sbprep_kernel(q_ref, k_ref, v_ref, qn_ref, kn_ref, qo_ref, ko_ref, vo_ref):
    qn = qn_ref[...]
    kn = kn_ref[...]
    for h in range(SB_HEADS):
        cols = slice(h * SB_HEAD_DIM, (h + 1) * SB_HEAD_DIM)
        qo_ref[:, cols] = (_rms(q_ref[:, cols], qn) * SB_SCALE).astype(BF16)
        ko_ref[:, cols] = _rms(k_ref[:, cols], kn).astype(BF16)
    vo_ref[...] = v_ref[...].astype(BF16)


def _sbprep(proj, qn_w, kn_w):
    t = proj.shape[0]
    tm = _row_tile(t, 512)
    spec = lambda cb: pl.BlockSpec((tm, SB_WIDTH), lambda i, cb=cb: (i, cb))
    wspec = pl.BlockSpec((1, SB_HEAD_DIM), lambda i: (0, 0))
    out = jax.ShapeDtypeStruct((t, SB_WIDTH), BF16)
    c0 = 4 * HG_WIDTH // SB_WIDTH
    return pl.pallas_call(
        _sbprep_kernel,
        grid=(t // tm,),
        in_specs=[spec(c0), spec(c0 + 1), spec(c0 + 2), wspec, wspec],
        out_specs=[spec(0), spec(0), spec(0)],
        out_shape=[out, out, out],
        compiler_params=_cparams("parallel"),
        name="sbprep",
    )(proj, proj, proj, qn_w.reshape(1, -1), kn_w.reshape(1, -1))


def _sbattn_kernel(q_ref, k_ref, v_ref, on_ref, o_ref, z_ref, p_ref, acc_ref, cb_ref, *, tq):
    i = pl.program_id(2)
    groups = tq // SB_BLOCK
    heads = [slice(h * SB_HEAD_DIM, (h + 1) * SB_HEAD_DIM) for h in range(SB_HPS)]
    qs = [q_ref[:, hc] for hc in heads]
    row = lax.broadcasted_iota(jnp.int32, (tq, SB_BLOCK), 0)
    col = lax.broadcasted_iota(jnp.int32, (tq, SB_BLOCK), 1)
    mr = lax.broadcasted_iota(jnp.int32, (SB_BLOCK, SB_BLOCK), 0)
    mc = lax.broadcasted_iota(jnp.int32, (SB_BLOCK, SB_BLOCK), 1)
    m1 = jnp.where(mr >= mc, -1.0, 0.0).astype(BF16)

    hs = range(SB_HPS)

    def slab_rows(kb):
        return pl.ds(pl.multiple_of(kb * tq, SB_BLOCK), tq)

    def scores(kb, slot):
        rows = slab_rows(kb)
        for h in hs:
            z_ref[slot, h] = _dot_t(qs[h], k_ref[rows, heads[h]])

    def apply_v(kb):
        rows = slab_rows(kb)
        for h in hs:
            acc_ref[h] += _dot(p_ref[h], v_ref[rows, heads[h]])

    def weights(slot, diagonal):
        for u in range(groups - 1, -1, -1):
            gcols = slice(u * SB_BLOCK, (u + 1) * SB_BLOCK)
            qrows = slice(u * SB_BLOCK if diagonal else 0, tq)
            if diagonal:
                keep = (col + u * SB_BLOCK < row)[qrows]
            zus, rs = [], []
            for h in hs:
                zu = z_ref[slot, h, qrows, gcols]
                sp = jnp.maximum(zu, 0.0) + jnp.log(1.0 + jnp.exp2(jnp.abs(zu) * -LOG2E))
                if diagonal:
                    sp = jnp.where(keep, sp, 0.0)
                zus.append(zu)
                rs.append(_dot(sp.astype(BF16), m1))
            for h in hs:
                r = rs[h]
                cb = cb_ref[h, qrows, :]
                a = jnp.exp(zus[h] + r + cb)
                if diagonal:
                    a = jnp.where(keep, a, 0.0)
                    if u > 0:
                        p_ref[h, 0:u * SB_BLOCK, gcols] = jnp.zeros((u * SB_BLOCK, SB_BLOCK), BF16)
                p_ref[h, qrows, gcols] = a.astype(BF16)
                cb_ref[h, qrows, :] = cb + jnp.broadcast_to(r[:, 0:1], r.shape)

    acc_ref[...] = jnp.zeros_like(acc_ref)
    cb_ref[...] = jnp.zeros_like(cb_ref)
    scores(i, 0)
    scores(jnp.maximum(i - 1, 0), 1)
    weights(0, True)

    def trip(jj, cur):
        kb = i - 1 - jj
        apply_v(kb + 1)
        scores(jnp.maximum(kb - 1, 0), 1 - cur)
        weights(cur, False)

    def two_trips(m, carry):
        trip(2 * m, 1)
        trip(2 * m + 1, 0)
        return carry

    lax.fori_loop(0, i // 2, two_trips, 0)

    @pl.when(i % 2 == 1)
    def _():
        trip(i - 1, 1)

    apply_v(0)
    for h in hs:
        o_ref[:, heads[h]] = _rms(acc_ref[h], on_ref[...]).astype(o_ref.dtype)


def _sbattn(q, k, v, on_w, batch, seq):
    tq = 3 * SB_BLOCK if seq % (3 * SB_BLOCK) == 0 else SB_BLOCK
    nq = seq // tq
    blk = pl.BlockSpec((tq, SB_HPS * SB_HEAD_DIM), lambda b, h, i: (b * nq + i, h))
    full = pl.BlockSpec((seq, SB_HPS * SB_HEAD_DIM), lambda b, h, i: (b, h))
    return pl.pallas_call(
        functools.partial(_sbattn_kernel, tq=tq),
        grid=(batch, SB_HEADS // SB_HPS, nq),
        in_specs=[blk, full, full, pl.BlockSpec((1, SB_HEAD_DIM), lambda b, h, i: (0, 0))],
        out_specs=blk,
        out_shape=jax.ShapeDtypeStruct((batch * seq, SB_WIDTH), BF16),
        scratch_shapes=[pltpu.VMEM((2, SB_HPS, tq, tq), F32), pltpu.VMEM((SB_HPS, tq, tq), BF16),
                        pltpu.VMEM((SB_HPS, tq, SB_HEAD_DIM), F32), pltpu.VMEM((SB_HPS, tq, SB_HEAD_DIM), F32)],
        compiler_params=_cparams("parallel", "parallel", "arbitrary"),
        name="sbattn",
    )(q, k, v, on_w.reshape(1, -1))


def _seg_matrix(n, seg):
    r = lax.broadcasted_iota(jnp.int32, (n, n), 0) // seg
    c = lax.broadcasted_iota(jnp.int32, (n, n), 1) // seg
    return (r == c).astype(BF16)


def _rwprep_kernel(*refs, layer):
    if layer > 0:
        (r_ref, k_ref, v_ref, lo_ref, mu_ref, w0_ref, w2_ref, a0_ref, a2_ref, g2_ref, kk_ref, ka_ref,
         v0_ref, v2_ref, vf_ref,
         ro_ref, wo_ref, ko_ref, vo_ref, kko_ref, bo_ref, go_ref, prev_ref) = refs
    else:
        (r_ref, k_ref, v_ref, lo_ref, mu_ref, w0_ref, w2_ref, a0_ref, a2_ref, g2_ref, kk_ref, ka_ref,
         ro_ref, wo_ref, ko_ref, vo_ref, kko_ref, bo_ref, go_ref, prev_ref) = refs

    @pl.when(pl.program_id(1) == 0)
    def _():
        prev_ref[...] = jnp.zeros_like(prev_ref)

    n = r_ref.shape[0]
    first = lax.broadcasted_iota(jnp.int32, (n, 1), 0) == 0

    def shifted(ref, part):
        p = ref[...]
        cols = slice(part * RW_WIDTH, (part + 1) * RW_WIDTH)
        prev = jnp.where(first, prev_ref[0:1, cols], pltpu.roll(p, 1, axis=0))
        prev_ref[0:1, cols] = p[n - 1:n, :]
        return p + (prev - p) * mu_ref[:, cols]

    r = shifted(r_ref, 0)
    k = shifted(k_ref, 1)
    v = shifted(v_ref, 2)
    lo = shifted(lo_ref, 3)
    lo_b = lo.astype(BF16)

    wl = -(w0_ref[...] + _dot(jnp.tanh(lo).astype(BF16), w2_ref[...]))
    w = -(jnp.maximum(wl, 0.0) + jnp.log1p(jnp.exp(-jnp.abs(wl)))) - 0.5
    decay = jnp.exp(-jnp.exp(w))
    a = jax.nn.sigmoid(a0_ref[...] + _dot(lo_b, a2_ref[...]))
    g = _dot(jax.nn.sigmoid(lo).astype(BF16), g2_ref[...])
    if layer > 0:
        v_first = jnp.concatenate([vf_ref[p] for p in range(RW_PAIRS)], axis=1)
        v = v + (v_first - v) * jax.nn.sigmoid(v0_ref[...] + _dot(lo_b, v2_ref[...]))

    seg = _seg_matrix(RW_WIDTH, RW_HEAD_DIM)
    kk = k * kk_ref[...]
    kk = kk * lax.rsqrt(jnp.maximum(_dot3(kk * kk, seg), 1e-24))
    k = k * (1.0 + (a - 1.0) * ka_ref[...])
    b = kk * a

    for p in range(RW_PAIRS):
        cols = slice(p * LANES, (p + 1) * LANES)
        ro_ref[p] = r[:, cols]
        wo_ref[p] = decay[:, cols]
        ko_ref[p] = k[:, cols]
        vo_ref[p] = v[:, cols]
        kko_ref[p] = kk[:, cols]
        bo_ref[p] = b[:, cols]
        go_ref[p] = g[:, cols]


def _rwprep(proj, mu, w0, w2p, a0, a2p, g2p, k_k, k_a, v_res, layer, batch, seq):
    lb_rows = _row_tile(seq, 256)
    nblk = seq // lb_rows
    c0 = RW_START // RW_WIDTH
    spec = lambda cb: pl.BlockSpec((lb_rows, RW_WIDTH), lambda b, i, cb=cb: (b * nblk + i, cb))
    pspec = pl.BlockSpec((RW_PAIRS, lb_rows, LANES), lambda b, i: (0, b * nblk + i, 0))
    vec = pl.BlockSpec((1, RW_WIDTH), lambda b, i: (0, 0))
    mat = pl.BlockSpec((RW_LORA_W, RW_WIDTH), lambda b, i: (0, 0))
    in_specs = [spec(c0), spec(c0 + 1), spec(c0 + 2), spec(c0 + 3),
                pl.BlockSpec((1, 4 * RW_WIDTH), lambda b, i: (0, 0)),
                vec, mat, vec, mat, mat, vec, vec]
    args = [proj, proj, proj, proj, mu, w0, w2p, a0, a2p, g2p, k_k, k_a]
    if layer > 0:
        v0, v2p, v_first = v_res
        in_specs += [vec, mat, pspec]
        args += [v0, v2p, v_first]
    out = jax.ShapeDtypeStruct((RW_PAIRS, batch * seq, LANES), F32)
    return pl.pallas_call(
        functools.partial(_rwprep_kernel, layer=layer),
        grid=(batch, nblk),
        in_specs=in_specs,
        out_specs=[pspec] * 7,
        out_shape=[out] * 7,
        scratch_shapes=[pltpu.VMEM((8, 4 * RW_WIDTH), F32)],
        compiler_params=_cparams("parallel", "arbitrary"),
        name="rwprep",
    )(*args)


def _rwscan_kernel(r_ref, w_ref, k_ref, v_ref, kk_ref, b_ref, y_ref, s_ref, *, batch, steps):
    @pl.when(pl.program_id(0) == 0)
    def _():
        s_ref[...] = jnp.zeros_like(s_ref)

    seg = _seg_matrix(LANES, RW_HEAD_DIM)
    vi = lax.broadcasted_iota(jnp.int32, (RW_HEAD_DIM, LANES), 0)
    li = lax.broadcasted_iota(jnp.int32, (RW_HEAD_DIM, LANES), 1)
    diag = (li % RW_HEAD_DIM) == vi
    n = RW_HEAD_DIM

    chains = [(p, bi) for p in range(RW_PAIRS) for bi in range(batch)]

    def y_row(yb):
        return jnp.sum(jnp.where(diag, yb, 0.0), axis=0, keepdims=True)

    def step(t, carry):
        row = pl.ds(t, 1)
        prow = pl.ds(jnp.maximum(t - 1, 0), 1)
        states, res = [], []
        for c, (p, bi) in enumerate(chains):
            s = s_ref[c]
            p1 = (s * kk_ref[p, bi, row, :]).astype(BF16)
            a = jnp.where(diag, v_ref[p, bi, row, :], 0.0).astype(BF16)
            p3 = (s * r_ref[p, bi, prow, :]).astype(BF16)
            states.append(s)
            res.append(_dot(jnp.concatenate([p1, a, p3], axis=0), seg))
        for c, (p, bi) in enumerate(chains):
            rc = res[c]
            sa = rc[0:n]
            vb = rc[n:2 * n]
            y_ref[p, bi, prow, :] = y_row(rc[2 * n:3 * n])
            s_ref[c] = (states[c] * w_ref[p, bi, row, :] - sa * b_ref[p, bi, row, :]
                        + vb * k_ref[p, bi, row, :])
        return carry

    def steps_unrolled(it, carry):
        for u in range(RW_UNROLL):
            step(it * RW_UNROLL + u, carry)
        return carry

    lax.fori_loop(0, steps // RW_UNROLL, steps_unrolled, 0)

    last = pl.ds(steps - 1, 1)
    for c, (p, bi) in enumerate(chains):
        yb = _dot((s_ref[c] * r_ref[p, bi, last, :]).astype(BF16), seg)
        y_ref[p, bi, last, :] = y_row(yb)


def _rwscan(r, w, k, v, kk, b, batch, seq):
    tb = _row_tile(seq, 128)
    spec = pl.BlockSpec((RW_PAIRS, batch, tb, LANES), lambda i: (0, 0, i, 0))
    shp = lambda x: x.reshape(RW_PAIRS, batch, seq, LANES)
    y = pl.pallas_call(
        functools.partial(_rwscan_kernel, batch=batch, steps=tb),
        grid=(seq // tb,),
        in_specs=[spec] * 6,
        out_specs=spec,
        out_shape=jax.ShapeDtypeStruct((RW_PAIRS, batch, seq, LANES), F32),
        scratch_shapes=[pltpu.VMEM((batch * RW_PAIRS, RW_HEAD_DIM, LANES), F32)],
        compiler_params=_cparams("arbitrary"),
        name="rwscan",
    )(shp(r), shp(w), shp(k), shp(v), shp(kk), shp(b))
    return y.reshape(RW_PAIRS, batch * seq, LANES)


def _rwpost_kernel(y_ref, r_ref, k_ref, v_ref, g_ref, lnw_ref, lnb_ref, rk_ref, o_ref):
    seg = _seg_matrix(LANES, RW_HEAD_DIM)
    y = y_ref[...]
    mu = _dot3(y, seg) * (1.0 / RW_HEAD_DIM)
    d = y - mu
    var = _dot3(d * d, seg) * (1.0 / RW_HEAD_DIM)
    yn = d * lax.rsqrt(var + RW_LN_EPS) * lnw_ref[...] + lnb_ref[...]
    bonus = _dot3(r_ref[...] * k_ref[...] * rk_ref[...], seg) * v_ref[...]
    o_ref[...] = ((yn + bonus) * g_ref[...]).astype(o_ref.dtype)


def _rwpost(y, r, k, v, g, ln_w, ln_b, r_k):
    t = y.shape[1]
    tm = _row_tile(t, 1024)
    row = pl.BlockSpec((None, tm, LANES), lambda p, i: (p, i, 0))
    vec = pl.BlockSpec((1, LANES), lambda p, i: (0, p))
    return pl.pallas_call(
        _rwpost_kernel,
        grid=(RW_PAIRS, t // tm),
        in_specs=[row] * 5 + [vec] * 3,
        out_specs=pl.BlockSpec((tm, LANES), lambda p, i: (i, p)),
        out_shape=jax.ShapeDtypeStruct((t, RW_WIDTH), BF16),
        compiler_params=_cparams("parallel", "parallel"),
        name="rwpost",
    )(y, r, k, v, g, ln_w, ln_b, r_k)


def _pad_rows(w, start):
    return jnp.pad(w, ((start, RW_LORA_W - start - w.shape[0]), (0, 0))).astype(BF16)


def _trunk(h, batch, seq, p, layers):
    vec = lambda x: x.reshape(1, -1).astype(F32)
    ffn1_wi, ffn1_wo = p["ffn1_wi"].astype(BF16), p["ffn1_wo"].astype(BF16)
    ffn2_wi, ffn2_wo = p["ffn2_wi"].astype(BF16), p["ffn2_wo"].astype(BF16)
    w_out = p["w_out"].astype(BF16)
    w_in_v = jnp.pad(p["w_in_v"], ((1, 0), (0, 0), (0, 0)))
    w_in = jnp.concatenate([p["w_in"], w_in_v], axis=2).astype(BF16)
    mu_all = jnp.concatenate([p["rw_mu"], jnp.pad(p["rw_mu_v"], ((1, 0), (0, 0)))], axis=1)
    v_first = None
    for l in layers:
        h = _ffn(h, p["norm_ffn1"][l], ffn1_wi, ffn1_wo, l)

        mu = mu_all[l]
        proj = _inproj(h, p["norm_mix"][l], w_in, l)

        o_hg = _hgrn2(proj, p["hg_lb"].astype(F32), p["hg_norm"][l], l, batch, seq)

        q, k, v = _sbprep(proj, p["sb_qn"][l], p["sb_kn"][l])
        o_sb = _sbattn(q, k, v, p["sb_on"][l], batch, seq)

        v_res = None
        if l > 0:
            v_res = (vec(p["rw_v0"][l - 1]),
                     _pad_rows(p["rw_v2"][l - 1], RW_DECAY_LORA + RW_AAA_LORA + RW_GATE_LORA), v_first)
        r, w, k_r, v_r, kk, b, g = _rwprep(
            proj, vec(mu), vec(p["rw_w0"][l]), _pad_rows(p["rw_w2"][l], 0),
            vec(p["rw_a0"][l]), _pad_rows(p["rw_a2"][l], RW_DECAY_LORA),
            _pad_rows(p["rw_g2"][l], RW_DECAY_LORA + RW_AAA_LORA),
            vec(p["rw_kk"][l]), vec(p["rw_ka"][l]), v_res, l, batch, seq)
        if l == 0:
            v_first = v_r
        y = _rwscan(r, w, k_r, v_r, kk, b, batch, seq)
        o_rw = _rwpost(y, r, k_r, v_r, g, vec(p["rw_ln_w"][l]), vec(p["rw_ln_b"][l]), vec(p["rw_rk"][l]))

        h = _outproj(h, o_hg, o_sb, o_rw, w_out, l)

        h = _ffn(h, p["norm_ffn2"][l], ffn2_wi, ffn2_wo, l)
    return h


def kernel(x, meta, norm_ffn1, ffn1_wi, ffn1_wo, norm_mix, w_in, w_in_v, hg_lb, hg_norm, sb_qn, sb_kn, sb_on, rw_mu, rw_mu_v, rw_w0, rw_w2, rw_a0, rw_a2, rw_g2, rw_v0, rw_v2, rw_kk, rw_ka, rw_rk, rw_ln_w, rw_ln_b, w_out, norm_ffn2, ffn2_wi, ffn2_wo):
    batch, s, d = x.shape
    l_real = N_META + s
    pad = (-l_real) % SB_BLOCK
    seq = l_real + pad
    h = jnp.concatenate([jnp.broadcast_to(meta.astype(x.dtype)[None], (batch, N_META, d)), x], axis=1)
    h = jnp.pad(h, ((0, 0), (0, pad), (0, 0))).reshape(batch * seq, d)
    p = dict(norm_ffn1=norm_ffn1, ffn1_wi=ffn1_wi, ffn1_wo=ffn1_wo, norm_mix=norm_mix, w_in=w_in,
             w_in_v=w_in_v, hg_lb=hg_lb, hg_norm=hg_norm, sb_qn=sb_qn, sb_kn=sb_kn, sb_on=sb_on,
             rw_mu=rw_mu, rw_mu_v=rw_mu_v, rw_w0=rw_w0, rw_w2=rw_w2, rw_a0=rw_a0, rw_a2=rw_a2,
             rw_g2=rw_g2, rw_v0=rw_v0, rw_v2=rw_v2, rw_kk=rw_kk, rw_ka=rw_ka, rw_rk=rw_rk,
             rw_ln_w=rw_ln_w, rw_ln_b=rw_ln_b, w_out=w_out, norm_ffn2=norm_ffn2, ffn2_wi=ffn2_wi,
             ffn2_wo=ffn2_wo)
    h = _trunk(h, batch, seq, p, range(DEPTH))
    return h.reshape(batch, seq, d)[:, N_META:l_real]
```

```python
import functools

import jax
import jax.numpy as jnp
from jax import lax
from jax.experimental import pallas as pl
from jax.experimental.pallas import tpu as pltpu

F32 = jnp.float32
BF16 = jnp.bfloat16

D_MODEL = 2048
DEPTH = 4
CHUNK = 64
N_META = 16
SB_BLOCK = 128
NORM_EPS = 1e-6
D_FF = 256 * ((8 * D_MODEL // 3 + 255) // 256)

HG_KEY = 128
HG_WIDTH = D_MODEL // 4
HG_HEADS = HG_WIDTH // HG_KEY

SB_HEAD_DIM = 128
SB_WIDTH = D_MODEL // 2
SB_HEADS = SB_WIDTH // SB_HEAD_DIM
SB_SCALE = SB_HEAD_DIM ** -0.5
SB_HPS = 2

RW_HEAD_DIM = 64
RW_WIDTH = D_MODEL - HG_WIDTH - SB_WIDTH
RW_HEADS = RW_WIDTH // RW_HEAD_DIM
RW_PAIRS = RW_HEADS // 2
RW_UNROLL = 8
RW_DECAY_LORA = max(32, int(round(1.8 * D_MODEL ** 0.5 / 32)) * 32)
RW_AAA_LORA = max(32, int(round(1.8 * D_MODEL ** 0.5 / 32)) * 32)
RW_MV_LORA = max(32, int(round(1.3 * D_MODEL ** 0.5 / 32)) * 32)
RW_GATE_LORA = max(32, int(round(0.6 * D_MODEL ** 0.8 / 32)) * 32)
RW_LN_EPS = 64e-5
RW_LORA_W = RW_DECAY_LORA + RW_AAA_LORA + RW_GATE_LORA + RW_MV_LORA

N_IN = 4 * HG_WIDTH + 3 * SB_WIDTH + 3 * RW_WIDTH + RW_DECAY_LORA + RW_AAA_LORA + RW_GATE_LORA
N_PROJ = N_IN + RW_MV_LORA
RW_START = 4 * HG_WIDTH + 3 * SB_WIDTH

LANES = 128
VMEM_LIMIT = 56 * 1024 * 1024
NEG_BIG = -1e30
LOG2E = 1.4426950408889634


def _cparams(*sem):
    return pltpu.CompilerParams(dimension_semantics=sem, vmem_limit_bytes=VMEM_LIMIT)


def _row_tile(t, want):
    tm = min(want, t)
    while t % tm:
        tm //= 2
    return tm


def _split2(x):
    hi = x.astype(BF16)
    lo = (x - hi.astype(F32)).astype(BF16)
    return hi, lo


def _split3(x):
    hi = x.astype(BF16)
    r = x - hi.astype(F32)
    mid = r.astype(BF16)
    lo = (r - mid.astype(F32)).astype(BF16)
    return hi, mid, lo


def _dot(a, b):
    return jnp.dot(a, b, preferred_element_type=F32)


def _dot_t(a, b):
    return lax.dot_general(a, b, (((1,), (1,)), ((), ())), preferred_element_type=F32)


def _dot3(x, m):
    hi, mid, lo = _split3(x)
    return _dot(hi, m) + _dot(mid, m) + _dot(lo, m)


def _log_sigmoid(z):
    return jnp.minimum(z, 0.0) - jnp.log1p(jnp.exp(-jnp.abs(z)))


def _rms(x, w):
    ms = jnp.mean(x * x, axis=-1, keepdims=True)
    return x * lax.rsqrt(ms + NORM_EPS) * w


def _ffn_kernel(x_ref, nw_ref, wg_ref, wu_ref, wo_ref, o_ref, xn_ref, acc_ref):
    j = pl.program_id(1)

    @pl.when(j == 0)
    def _():
        xn_ref[...] = _rms(x_ref[...], nw_ref[...]).astype(BF16)
        acc_ref[...] = jnp.zeros_like(acc_ref)

    xn = xn_ref[...]
    g = _dot(xn, wg_ref[...])
    u = _dot(xn, wu_ref[...])
    a = (g * jax.nn.sigmoid(g)) * u
    acc_ref[...] += _dot(a.astype(BF16), wo_ref[...])

    @pl.when(j == pl.num_programs(1) - 1)
    def _():
        o_ref[...] = x_ref[...] + 0.5 * acc_ref[...]


def _ffn(h, nw, wi, wo, layer):
    t = h.shape[0]
    tm = _row_tile(t, 512)
    tf = 512
    nf = D_FF // tf
    return pl.pallas_call(
        _ffn_kernel,
        grid=(t // tm, nf),
        in_specs=[
            pl.BlockSpec((tm, D_MODEL), lambda i, j: (i, 0)),
            pl.BlockSpec((1, D_MODEL), lambda i, j: (0, 0)),
            pl.BlockSpec((None, D_MODEL, tf), lambda i, j: (layer, 0, j)),
            pl.BlockSpec((None, D_MODEL, tf), lambda i, j: (layer, 0, j + nf)),
            pl.BlockSpec((None, tf, D_MODEL), lambda i, j: (layer, j, 0)),
        ],
        out_specs=pl.BlockSpec((tm, D_MODEL), lambda i, j: (i, 0)),
        out_shape=jax.ShapeDtypeStruct((t, D_MODEL), F32),
        scratch_shapes=[pltpu.VMEM((tm, D_MODEL), BF16), pltpu.VMEM((tm, D_MODEL), F32)],
        compiler_params=_cparams("parallel", "arbitrary"),
        name="ffn",
    )(h, nw.reshape(1, D_MODEL), wi, wi, wo)


def _inproj_kernel(x_ref, nw_ref, w_ref, o_ref, xn_ref):
    @pl.when(pl.program_id(1) == 0)
    def _():
        xn_ref[...] = _rms(x_ref[...], nw_ref[...]).astype(BF16)

    o_ref[...] = _dot(xn_ref[...], w_ref[...])


def _inproj(h, nw, w, layer):
    t = h.shape[0]
    tm = 768 if t % 768 == 0 else _row_tile(t, 512)
    tn = N_PROJ // 4
    return pl.pallas_call(
        _inproj_kernel,
        grid=(t // tm, N_PROJ // tn),
        in_specs=[
            pl.BlockSpec((tm, D_MODEL), lambda i, j: (i, 0)),
            pl.BlockSpec((1, D_MODEL), lambda i, j: (0, 0)),
            pl.BlockSpec((None, D_MODEL, tn), lambda i, j: (layer, 0, j)),
        ],
        out_specs=pl.BlockSpec((tm, tn), lambda i, j: (i, j)),
        out_shape=jax.ShapeDtypeStruct((t, N_PROJ), F32),
        scratch_shapes=[pltpu.VMEM((tm, D_MODEL), BF16)],
        compiler_params=_cparams("parallel", "arbitrary"),
        name="inproj",
    )(h, nw.reshape(1, D_MODEL), w)


def _outproj_kernel(h_ref, a0_ref, a1_ref, a2_ref, w_ref, o_ref):
    acc = _dot(a0_ref[...], w_ref[0:HG_WIDTH, :])
    acc += _dot(a1_ref[...], w_ref[HG_WIDTH:HG_WIDTH + SB_WIDTH, :])
    acc += _dot(a2_ref[...], w_ref[HG_WIDTH + SB_WIDTH:, :])
    o_ref[...] = h_ref[...] + acc


def _outproj(h, a_hg, a_sb, a_rw, w_out, layer):
    t = h.shape[0]
    tm = _row_tile(t, 512)
    row = lambda i: (i, 0)
    return pl.pallas_call(
        _outproj_kernel,
        grid=(t // tm,),
        in_specs=[
            pl.BlockSpec((tm, D_MODEL), row),
            pl.BlockSpec((tm, HG_WIDTH), row),
            pl.BlockSpec((tm, SB_WIDTH), row),
            pl.BlockSpec((tm, RW_WIDTH), row),
            pl.BlockSpec((None, D_MODEL, D_MODEL), lambda i: (layer, 0, 0)),
        ],
        out_specs=pl.BlockSpec((tm, D_MODEL), row),
        out_shape=jax.ShapeDtypeStruct((t, D_MODEL), F32),
        compiler_params=_cparams("parallel"),
        name="outproj",
    )(h, a_hg, a_sb, a_rw, w_out)


HG_SUB = 16


def _hgrn2_kernel(q_ref, f_ref, i_ref, g_ref, lbp_ref, nw_ref, o_ref, s_ref, *, layer, n_chunks):
    @pl.when(pl.program_id(1) == 0)
    def _():
        s_ref[...] = jnp.zeros_like(s_ref)

    if layer > 0:
        p = lbp_ref[...]
        e = jnp.exp(p - jnp.max(p, axis=0, keepdims=True))
        sm = e / jnp.sum(e, axis=0, keepdims=True)
        lb_all = sm[1:2, :]
        for r in range(2, layer + 1):
            lb_all = lb_all + sm[r:r + 1, :]
        log_lb_all = jnp.log(lb_all)
        log_1mlb_all = jnp.log1p(-lb_all)

    row = lax.broadcasted_iota(jnp.int32, (CHUNK, CHUNK), 0)
    col = lax.broadcasted_iota(jnp.int32, (CHUNK, CHUNK), 1)
    tri = (row >= col).astype(BF16)
    srow = lax.broadcasted_iota(jnp.int32, (HG_SUB, 1), 0)
    scol = lax.broadcasted_iota(jnp.int32, (HG_SUB, CHUNK), 1)
    nw = nw_ref[...]
    n_sub = CHUNK // HG_SUB

    def chunk(c, carry):
        r0 = pl.multiple_of(c * CHUNK, CHUNK)
        rows = pl.ds(r0, CHUNK)
        hs = range(HG_HEADS)
        heads = [slice(h * HG_KEY, (h + 1) * HG_KEY) for h in hs]
        qs, ks, vbs, Gs = [], [], [], []
        for cols in heads:
            z = f_ref[rows, cols]
            ls = _log_sigmoid(z)
            if layer > 0:
                a = log_lb_all[:, cols]
                cc = log_1mlb_all[:, cols] + ls
                log_f = jnp.maximum(a, cc) + jnp.log1p(jnp.exp(-jnp.abs(a - cc)))
                k = (1.0 - lb_all[:, cols]) * jax.nn.sigmoid(-z)
            else:
                log_f = ls
                k = jax.nn.sigmoid(-z)
            qv = q_ref[rows, cols]
            q = qv * jax.nn.sigmoid(qv)
            v = i_ref[rows, cols]
            f_hi, f_mid, f_lo = _split3(log_f)
            Gs.append(_dot(tri, f_hi) + _dot(tri, f_mid) + _dot(tri, f_lo))
            qs.append(q)
            ks.append(k)
            vbs.append(v.astype(BF16))

        o_st, offs, pks = [], [], []
        for h in hs:
            q, k, G = qs[h], ks[h], Gs[h]
            kb = k.astype(BF16)
            g_end = G[CHUNK - 1:CHUNK, :]
            st = s_ref[h]
            o_st.append(_dot_t((q * jnp.exp(G)).astype(BF16), st.astype(BF16)))
            kd_end = (k * jnp.exp(g_end - G)).astype(BF16)
            upd = lax.dot_general(vbs[h], kd_end, (((0,), (0,)), ((), ())), preferred_element_type=F32)
            s_ref[h] = st * jnp.exp(g_end) + upd
            off_h, pk_h = [], []
            for i in range(n_sub):
                sl = slice(i * HG_SUB, (i + 1) * HG_SUB)
                G_i = G[sl, :]
                q_i = q[sl, :]
                if i > 0:
                    g_b = G[i * HG_SUB - 1:i * HG_SUB, :]
                    qd = (q_i * jnp.exp(G_i - g_b)).astype(BF16)
                    kd = (k * jnp.exp(jnp.minimum(g_b - G, 0.0))).astype(BF16)
                    off_h.append(_dot_t(qd, kd))
                else:
                    off_h.append(None)
                p_all = []
                for s in range(HG_SUB):
                    d = jnp.where(srow >= s, G_i - G_i[s:s + 1, :], NEG_BIG)
                    p_all.append((q_i * jnp.exp(d)).astype(BF16))
                pk_h.append(_dot_t(jnp.concatenate(p_all, axis=0), kb))
            offs.append(off_h)
            pks.append(pk_h)

        outs = []
        for h in hs:
            att_rows = []
            for i in range(n_sub):
                att_i = jnp.zeros((HG_SUB, CHUNK), F32)
                if i > 0:
                    att_i = jnp.where(scol < i * HG_SUB, offs[h][i], 0.0)
                for s in range(HG_SUB):
                    att_i = att_i + jnp.where(scol == i * HG_SUB + s,
                                              pks[h][i][s * HG_SUB:(s + 1) * HG_SUB], 0.0)
                att_rows.append(att_i)
            att = jnp.concatenate(att_rows, axis=0)
            outs.append(o_st[h] + _dot(att.astype(BF16), vbs[h]))

        for h, cols in enumerate(heads):
            gv = g_ref[rows, cols]
            o_ref[rows, cols] = (_rms(outs[h], nw) * (gv * jax.nn.sigmoid(gv))).astype(o_ref.dtype)
        return carry

    lax.fori_loop(0, n_chunks, chunk, 0)


def _hgrn2(proj, hg_lb, norm_w, layer, batch, seq):
    lb_rows = 6 * CHUNK if seq % (6 * CHUNK) == 0 else _row_tile(seq, 256)
    nblk = seq // lb_rows
    spec = lambda cb: pl.BlockSpec((lb_rows, HG_WIDTH), lambda b, i, cb=cb: (b * nblk + i, cb))
    return pl.pallas_call(
        functools.partial(_hgrn2_kernel, layer=layer, n_chunks=lb_rows // CHUNK),
        grid=(batch, nblk),
        in_specs=[spec(0), spec(1), spec(2), spec(3),
                  pl.BlockSpec((DEPTH, HG_WIDTH), lambda b, i: (0, 0)),
                  pl.BlockSpec((1, HG_KEY), lambda b, i: (0, 0))],
        out_specs=pl.BlockSpec((lb_rows, HG_WIDTH), lambda b, i: (b * nblk + i, 0)),
        out_shape=jax.ShapeDtypeStruct((batch * seq, HG_WIDTH), BF16),
        scratch_shapes=[pltpu.VMEM((HG_HEADS, HG_KEY, HG_KEY), F32)],
        compiler_params=_cparams("parallel", "arbitrary"),
        name="hgrn2",
    )(proj, proj, proj, proj, hg_lb, norm_w.reshape(1, HG_KEY))


def _sbprep_kernel(q_ref, k_ref, v_ref, qn_ref, kn_ref, qo_ref, ko_ref, vo_ref):
    qn = qn_ref[...]
    kn = kn_ref[...]
    for h in range(SB_HEADS):
        cols = slice(h * SB_HEAD_DIM, (h + 1) * SB_HEAD_DIM)
        qo_ref[:, cols] = (_rms(q_ref[:, cols], qn) * SB_SCALE).astype(BF16)
        ko_ref[:, cols] = _rms(k_ref[:, cols], kn).astype(BF16)
    vo_ref[...] = v_ref[...].astype(BF16)


def _sbprep(proj, qn_w, kn_w):
    t = proj.shape[0]
    tm = _row_tile(t, 512)
    spec = lambda cb: pl.BlockSpec((tm, SB_WIDTH), lambda i, cb=cb: (i, cb))
    wspec = pl.BlockSpec((1, SB_HEAD_DIM), lambda i: (0, 0))
    out = jax.ShapeDtypeStruct((t, SB_WIDTH), BF16)
    c0 = 4 * HG_WIDTH // SB_WIDTH
    return pl.pallas_call(
        _sbprep_kernel,
        grid=(t // tm,),
        in_specs=[spec(c0), spec(c0 + 1), spec(c0 + 2), wspec, wspec],
        out_specs=[spec(0), spec(0), spec(0)],
        out_shape=[out, out, out],
        compiler_params=_cparams("parallel"),
        name="sbprep",
    )(proj, proj, proj, qn_w.reshape(1, -1), kn_w.reshape(1, -1))


def _sbattn_kernel(q_ref, k_ref, v_ref, on_ref, o_ref, z_ref, p_ref, acc_ref, cb_ref, *, tq):
    i = pl.program_id(2)
    groups = tq // SB_BLOCK
    heads = [slice(h * SB_HEAD_DIM, (h + 1) * SB_HEAD_DIM) for h in range(SB_HPS)]
    qs = [q_ref[:, hc] for hc in heads]
    row = lax.broadcasted_iota(jnp.int32, (tq, SB_BLOCK), 0)
    col = lax.broadcasted_iota(jnp.int32, (tq, SB_BLOCK), 1)
    mr = lax.broadcasted_iota(jnp.int32, (SB_BLOCK, SB_BLOCK), 0)
    mc = lax.broadcasted_iota(jnp.int32, (SB_BLOCK, SB_BLOCK), 1)
    m1 = jnp.where(mr >= mc, -1.0, 0.0).astype(BF16)

    hs = range(SB_HPS)

    def slab_rows(kb):
        return pl.ds(pl.multiple_of(kb * tq, SB_BLOCK), tq)

    def scores(kb, slot):
        rows = slab_rows(kb)
        for h in hs:
            z_ref[slot, h] = _dot_t(qs[h], k_ref[rows, heads[h]])

    def apply_v(kb):
        rows = slab_rows(kb)
        for h in hs:
            acc_ref[h] += _dot(p_ref[h], v_ref[rows, heads[h]])

    def weights(slot, diagonal):
        for u in range(groups - 1, -1, -1):
            gcols = slice(u * SB_BLOCK, (u + 1) * SB_BLOCK)
            qrows = slice(u * SB_BLOCK if diagonal else 0, tq)
            if diagonal:
                keep = (col + u * SB_BLOCK < row)[qrows]
            zus, rs = [], []
            for h in hs:
                zu = z_ref[slot, h, qrows, gcols]
                sp = jnp.maximum(zu, 0.0) + jnp.log(1.0 + jnp.exp2(jnp.abs(zu) * -LOG2E))
                if diagonal:
                    sp = jnp.where(keep, sp, 0.0)
                zus.append(zu)
                rs.append(_dot(sp.astype(BF16), m1))
            for h in hs:
                r = rs[h]
                cb = cb_ref[h, qrows, :]
                a = jnp.exp(zus[h] + r + cb)
                if diagonal:
                    a = jnp.where(keep, a, 0.0)
                    if u > 0:
                        p_ref[h, 0:u * SB_BLOCK, gcols] = jnp.zeros((u * SB_BLOCK, SB_BLOCK), BF16)
                p_ref[h, qrows, gcols] = a.astype(BF16)
                cb_ref[h, qrows, :] = cb + jnp.broadcast_to(r[:, 0:1], r.shape)

    acc_ref[...] = jnp.zeros_like(acc_ref)
    cb_ref[...] = jnp.zeros_like(cb_ref)
    scores(i, 0)
    scores(jnp.maximum(i - 1, 0), 1)
    weights(0, True)

    def trip(jj, cur):
        kb = i - 1 - jj
        apply_v(kb + 1)
        scores(jnp.maximum(kb - 1, 0), 1 - cur)
        weights(cur, False)

    def two_trips(m, carry):
        trip(2 * m, 1)
        trip(2 * m + 1, 0)
        return carry

    lax.fori_loop(0, i // 2, two_trips, 0)

    @pl.when(i % 2 == 1)
    def _():
        trip(i - 1, 1)

    apply_v(0)
    for h in hs:
        o_ref[:, heads[h]] = _rms(acc_ref[h], on_ref[...]).astype(o_ref.dtype)


def _sbattn(q, k, v, on_w, batch, seq):
    tq = 3 * SB_BLOCK if seq % (3 * SB_BLOCK) == 0 else SB_BLOCK
    nq = seq // tq
    blk = pl.BlockSpec((tq, SB_HPS * SB_HEAD_DIM), lambda b, h, i: (b * nq + i, h))
    full = pl.BlockSpec((seq, SB_HPS * SB_HEAD_DIM), lambda b, h, i: (b, h))
    return pl.pallas_call(
        functools.partial(_sbattn_kernel, tq=tq),
        grid=(batch, SB_HEADS // SB_HPS, nq),
        in_specs=[blk, full, full, pl.BlockSpec((1, SB_HEAD_DIM), lambda b, h, i: (0, 0))],
        out_specs=blk,
        out_shape=jax.ShapeDtypeStruct((batch * seq, SB_WIDTH), BF16),
        scratch_shapes=[pltpu.VMEM((2, SB_HPS, tq, tq), F32), pltpu.VMEM((SB_HPS, tq, tq), BF16),
                        pltpu.VMEM((SB_HPS, tq, SB_HEAD_DIM), F32), pltpu.VMEM((SB_HPS, tq, SB_HEAD_DIM), F32)],
        compiler_params=_cparams("parallel", "parallel", "arbitrary"),
        name="sbattn",
    )(q, k, v, on_w.reshape(1, -1))


def _seg_matrix(n, seg):
    r = lax.broadcasted_iota(jnp.int32, (n, n), 0) // seg
    c = lax.broadcasted_iota(jnp.int32, (n, n), 1) // seg
    return (r == c).astype(BF16)


def _rwprep_kernel(*refs, layer):
    if layer > 0:
        (r_ref, k_ref, v_ref, lo_ref, mu_ref, w0_ref, w2_ref, a0_ref, a2_ref, g2_ref, kk_ref, ka_ref,
         v0_ref, v2_ref, vf_ref,
         ro_ref, wo_ref, ko_ref, vo_ref, kko_ref, bo_ref, go_ref, prev_ref) = refs
    else:
        (r_ref, k_ref, v_ref, lo_ref, mu_ref, w0_ref, w2_ref, a0_ref, a2_ref, g2_ref, kk_ref, ka_ref,
         ro_ref, wo_ref, ko_ref, vo_ref, kko_ref, bo_ref, go_ref, prev_ref) = refs

    @pl.when(pl.program_id(1) == 0)
    def _():
        prev_ref[...] = jnp.zeros_like(prev_ref)

    n = r_ref.shape[0]
    first = lax.broadcasted_iota(jnp.int32, (n, 1), 0) == 0

    def shifted(ref, part):
        p = ref[...]
        cols = slice(part * RW_WIDTH, (part + 1) * RW_WIDTH)
        prev = jnp.where(first, prev_ref[0:1, cols], pltpu.roll(p, 1, axis=0))
        prev_ref[0:1, cols] = p[n - 1:n, :]
        return p + (prev - p) * mu_ref[:, cols]

    r = shifted(r_ref, 0)
    k = shifted(k_ref, 1)
    v = shifted(v_ref, 2)
    lo = shifted(lo_ref, 3)
    lo_b = lo.astype(BF16)

    wl = -(w0_ref[...] + _dot(jnp.tanh(lo).astype(BF16), w2_ref[...]))
    w = -(jnp.maximum(wl, 0.0) + jnp.log1p(jnp.exp(-jnp.abs(wl)))) - 0.5
    decay = jnp.exp(-jnp.exp(w))
    a = jax.nn.sigmoid(a0_ref[...] + _dot(lo_b, a2_ref[...]))
    g = _dot(jax.nn.sigmoid(lo).astype(BF16), g2_ref[...])
    if layer > 0:
        v_first = jnp.concatenate([vf_ref[p] for p in range(RW_PAIRS)], axis=1)
        v = v + (v_first - v) * jax.nn.sigmoid(v0_ref[...] + _dot(lo_b, v2_ref[...]))

    seg = _seg_matrix(RW_WIDTH, RW_HEAD_DIM)
    kk = k * kk_ref[...]
    kk = kk * lax.rsqrt(jnp.maximum(_dot3(kk * kk, seg), 1e-24))
    k = k * (1.0 + (a - 1.0) * ka_ref[...])
    b = kk * a

    for p in range(RW_PAIRS):
        cols = slice(p * LANES, (p + 1) * LANES)
        ro_ref[p] = r[:, cols]
        wo_ref[p] = decay[:, cols]
        ko_ref[p] = k[:, cols]
        vo_ref[p] = v[:, cols]
        kko_ref[p] = kk[:, cols]
        bo_ref[p] = b[:, cols]
        go_ref[p] = g[:, cols]


def _rwprep(proj, mu, w0, w2p, a0, a2p, g2p, k_k, k_a, v_res, layer, batch, seq):
    lb_rows = _row_tile(seq, 256)
    nblk = seq // lb_rows
    c0 = RW_START // RW_WIDTH
    spec = lambda cb: pl.BlockSpec((lb_rows, RW_WIDTH), lambda b, i, cb=cb: (b * nblk + i, cb))
    pspec = pl.BlockSpec((RW_PAIRS, lb_rows, LANES), lambda b, i: (0, b * nblk + i, 0))
    vec = pl.BlockSpec((1, RW_WIDTH), lambda b, i: (0, 0))
    mat = pl.BlockSpec((RW_LORA_W, RW_WIDTH), lambda b, i: (0, 0))
    in_specs = [spec(c0), spec(c0 + 1), spec(c0 + 2), spec(c0 + 3),
                pl.BlockSpec((1, 4 * RW_WIDTH), lambda b, i: (0, 0)),
                vec, mat, vec, mat, mat, vec, vec]
    args = [proj, proj, proj, proj, mu, w0, w2p, a0, a2p, g2p, k_k, k_a]
    if layer > 0:
        v0, v2p, v_first = v_res
        in_specs += [vec, mat, pspec]
        args += [v0, v2p, v_first]
    out = jax.ShapeDtypeStruct((RW_PAIRS, batch * seq, LANES), F32)
    return pl.pallas_call(
        functools.partial(_rwprep_kernel, layer=layer),
        grid=(batch, nblk),
        in_specs=in_specs,
        out_specs=[pspec] * 7,
        out_shape=[out] * 7,
        scratch_shapes=[pltpu.VMEM((8, 4 * RW_WIDTH), F32)],
        compiler_params=_cparams("parallel", "arbitrary"),
        name="rwprep",
    )(*args)


def _rwscan_kernel(r_ref, w_ref, k_ref, v_ref, kk_ref, b_ref, y_ref, s_ref, *, batch, steps):
    @pl.when(pl.program_id(0) == 0)
    def _():
        s_ref[...] = jnp.zeros_like(s_ref)

    seg = _seg_matrix(LANES, RW_HEAD_DIM)
    vi = lax.broadcasted_iota(jnp.int32, (RW_HEAD_DIM, LANES), 0)
    li = lax.broadcasted_iota(jnp.int32, (RW_HEAD_DIM, LANES), 1)
    diag = (li % RW_HEAD_DIM) == vi
    n = RW_HEAD_DIM

    chains = [(p, bi) for p in range(RW_PAIRS) for bi in range(batch)]

    def y_row(yb):
        return jnp.sum(jnp.where(diag, yb, 0.0), axis=0, keepdims=True)

    def step(t, carry):
        row = pl.ds(t, 1)
        prow = pl.ds(jnp.maximum(t - 1, 0), 1)
        states, res = [], []
        for c, (p, bi) in enumerate(chains):
            s = s_ref[c]
            p1 = (s * kk_ref[p, bi, row, :]).astype(BF16)
            a = jnp.where(diag, v_ref[p, bi, row, :], 0.0).astype(BF16)
            p3 = (s * r_ref[p, bi, prow, :]).astype(BF16)
            states.append(s)
            res.append(_dot(jnp.concatenate([p1, a, p3], axis=0), seg))
        for c, (p, bi) in enumerate(chains):
            rc = res[c]
            sa = rc[0:n]
            vb = rc[n:2 * n]
            y_ref[p, bi, prow, :] = y_row(rc[2 * n:3 * n])
            s_ref[c] = (states[c] * w_ref[p, bi, row, :] - sa * b_ref[p, bi, row, :]
                        + vb * k_ref[p, bi, row, :])
        return carry

    def steps_unrolled(it, carry):
        for u in range(RW_UNROLL):
            step(it * RW_UNROLL + u, carry)
        return carry

    lax.fori_loop(0, steps // RW_UNROLL, steps_unrolled, 0)

    last = pl.ds(steps - 1, 1)
    for c, (p, bi) in enumerate(chains):
        yb = _dot((s_ref[c] * r_ref[p, bi, last, :]).astype(BF16), seg)
        y_ref[p, bi, last, :] = y_row(yb)


def _rwscan(r, w, k, v, kk, b, batch, seq):
    tb = _row_tile(seq, 128)
    spec = pl.BlockSpec((RW_PAIRS, batch, tb, LANES), lambda i: (0, 0, i, 0))
    shp = lambda x: x.reshape(RW_PAIRS, batch, seq, LANES)
    y = pl.pallas_call(
        functools.partial(_rwscan_kernel, batch=batch, steps=tb),
        grid=(seq // tb,),
        in_specs=[spec] * 6,
        out_specs=spec,
        out_shape=jax.ShapeDtypeStruct((RW_PAIRS, batch, seq, LANES), F32),
        scratch_shapes=[pltpu.VMEM((batch * RW_PAIRS, RW_HEAD_DIM, LANES), F32)],
        compiler_params=_cparams("arbitrary"),
        name="rwscan",
    )(shp(r), shp(w), shp(k), shp(v), shp(kk), shp(b))
    return y.reshape(RW_PAIRS, batch * seq, LANES)


def _rwpost_kernel(y_ref, r_ref, k_ref, v_ref, g_ref, lnw_ref, lnb_ref, rk_ref, o_ref):
    seg = _seg_matrix(LANES, RW_HEAD_DIM)
    y = y_ref[...]
    mu = _dot3(y, seg) * (1.0 / RW_HEAD_DIM)
    d = y - mu
    var = _dot3(d * d, seg) * (1.0 / RW_HEAD_DIM)
    yn = d * lax.rsqrt(var + RW_LN_EPS) * lnw_ref[...] + lnb_ref[...]
    bonus = _dot3(r_ref[...] * k_ref[...] * rk_ref[...], seg) * v_ref[...]
    o_ref[...] = ((yn + bonus) * g_ref[...]).astype(o_ref.dtype)


def _rwpost(y, r, k, v, g, ln_w, ln_b, r_k):
    t = y.shape[1]
    tm = _row_tile(t, 1024)
    row = pl.BlockSpec((None, tm, LANES), lambda p, i: (p, i, 0))
    vec = pl.BlockSpec((1, LANES), lambda p, i: (0, p))
    return pl.pallas_call(
        _rwpost_kernel,
        grid=(RW_PAIRS, t // tm),
        in_specs=[row] * 5 + [vec] * 3,
        out_specs=pl.BlockSpec((tm, LANES), lambda p, i: (i, p)),
        out_shape=jax.ShapeDtypeStruct((t, RW_WIDTH), BF16),
        compiler_params=_cparams("parallel", "parallel"),
        name="rwpost",
    )(y, r, k, v, g, ln_w, ln_b, r_k)


def _pad_rows(w, start):
    return jnp.pad(w, ((start, RW_LORA_W - start - w.shape[0]), (0, 0))).astype(BF16)


def _trunk(h, batch, seq, p, layers):
    vec = lambda x: x.reshape(1, -1).astype(F32)
    ffn1_wi, ffn1_wo = p["ffn1_wi"].astype(BF16), p["ffn1_wo"].astype(BF16)
    ffn2_wi, ffn2_wo = p["ffn2_wi"].astype(BF16), p["ffn2_wo"].astype(BF16)
    w_out = p["w_out"].astype(BF16)
    w_in_v = jnp.pad(p["w_in_v"], ((1, 0), (0, 0), (0, 0)))
    w_in = jnp.concatenate([p["w_in"], w_in_v], axis=2).astype(BF16)
    mu_all = jnp.concatenate([p["rw_mu"], jnp.pad(p["rw_mu_v"], ((1, 0), (0, 0)))], axis=1)
    v_first = None
    for l in layers:
        h = _ffn(h, p["norm_ffn1"][l], ffn1_wi, ffn1_wo, l)

        mu = mu_all[l]
        proj = _inproj(h, p["norm_mix"][l], w_in, l)

        o_hg = _hgrn2(proj, p["hg_lb"].astype(F32), p["hg_norm"][l], l, batch, seq)

        q, k, v = _sbprep(proj, p["sb_qn"][l], p["sb_kn"][l])
        o_sb = _sbattn(q, k, v, p["sb_on"][l], batch, seq)

        v_res = None
        if l > 0:
            v_res = (vec(p["rw_v0"][l - 1]),
                     _pad_rows(p["rw_v2"][l - 1], RW_DECAY_LORA + RW_AAA_LORA + RW_GATE_LORA), v_first)
        r, w, k_r, v_r, kk, b, g = _rwprep(
            proj, vec(mu), vec(p["rw_w0"][l]), _pad_rows(p["rw_w2"][l], 0),
            vec(p["rw_a0"][l]), _pad_rows(p["rw_a2"][l], RW_DECAY_LORA),
            _pad_rows(p["rw_g2"][l], RW_DECAY_LORA + RW_AAA_LORA),
            vec(p["rw_kk"][l]), vec(p["rw_ka"][l]), v_res, l, batch, seq)
        if l == 0:
            v_first = v_r
        y = _rwscan(r, w, k_r, v_r, kk, b, batch, seq)
        o_rw = _rwpost(y, r, k_r, v_r, g, vec(p["rw_ln_w"][l]), vec(p["rw_ln_b"][l]), vec(p["rw_rk"][l]))

        h = _outproj(h, o_hg, o_sb, o_rw, w_out, l)

        h = _ffn(h, p["norm_ffn2"][l], ffn2_wi, ffn2_wo, l)
    return h


def kernel(x, meta, norm_ffn1, ffn1_wi, ffn1_wo, norm_mix, w_in, w_in_v, hg_lb, hg_norm, sb_qn, sb_kn, sb_on, rw_mu, rw_mu_v, rw_w0, rw_w2, rw_a0, rw_a2, rw_g2, rw_v0, rw_v2, rw_kk, rw_ka, rw_rk, rw_ln_w, rw_ln_b, w_out, norm_ffn2, ffn2_wi, ffn2_wo):
    batch, s, d = x.shape
    l_real = N_META + s
    pad = (-l_real) % SB_BLOCK
    seq = l_real + pad
    h = jnp.concatenate([jnp.broadcast_to(meta.astype(x.dtype)[None], (batch, N_META, d)), x], axis=1)
    h = jnp.pad(h, ((0, 0), (0, pad), (0, 0))).reshape(batch * seq, d)
    p = dict(norm_ffn1=norm_ffn1, ffn1_wi=ffn1_wi, ffn1_wo=ffn1_wo, norm_mix=norm_mix, w_in=w_in,
             w_in_v=w_in_v, hg_lb=hg_lb, hg_norm=hg_norm, sb_qn=sb_qn, sb_kn=sb_kn, sb_on=sb_on,
             rw_mu=rw_mu, rw_mu_v=rw_mu_v, rw_w0=rw_w0, rw_w2=rw_w2, rw_a0=rw_a0, rw_a2=rw_a2,
             rw_g2=rw_g2, rw_v0=rw_v0, rw_v2=rw_v2, rw_kk=rw_kk, rw_ka=rw_ka, rw_rk=rw_rk,
             rw_ln_w=rw_ln_w, rw_ln_b=rw_ln_b, w_out=w_out, norm_ffn2=norm_ffn2, ffn2_wi=ffn2_wi,
             ffn2_wo=ffn2_wo)
    h = _trunk(h, batch, seq, p, range(DEPTH))
    return h.reshape(batch, seq, d)[:, N_META:l_real]
```
